```python
import jax
import jax.numpy as jnp
from jax import lax
import numpy as np


D_MODEL = 1024
BATCH = 4
SEQ = 8192
DEPTH = 2

DEEPNORM_ALPHA = (2.0 * DEPTH) ** 0.25
DEEPNORM_BETA = (8.0 * DEPTH) ** -0.25
LN_EPS = 1e-5
RMS_EPS = 1e-6

A_HEADS = 4
A_DK = 128
A_DV = 128
A_FDIM = A_HEADS * A_DK
A_WIDTH = A_HEADS * A_DV
A_CHUNK = 64

B_WINDOWS = (2, 4, 8, 16)
B_GROUPS = len(B_WINDOWS)
B_GROUP_DIM = 128
B_WIDTH = B_GROUPS * B_GROUP_DIM

EVEN_SPLITS = (A_FDIM, 2 * A_FDIM, 2 * A_FDIM + A_WIDTH, 2 * A_FDIM + 2 * A_WIDTH)
EVEN_IN_COLS = 2 * A_FDIM + 2 * A_WIDTH + B_WIDTH
EVEN_MIX_WIDTH = A_WIDTH + B_WIDTH

C_HEADS = 16
C_HEAD_DIM = 64
C_WIDTH = C_HEADS * C_HEAD_DIM
C_QBLOCK = 128
ODD_IN_COLS = 3 * C_WIDTH

N_EXPERTS = 32
TOP_K = 4
D_FF = D_MODEL
SWIGLU_ALPHA = 1.702
SWIGLU_LIMIT = 7.0

kernel_name = 'hybrid_hgrn2_pool_stickbreak_moe_deepnorm'


def layer_norm(x, w, b):
    xf = x.astype(jnp.float32)
    mu = jnp.mean(xf, axis=-1, keepdims=True)
    var = jnp.mean(jnp.square(xf - mu), axis=-1, keepdims=True)
    return (xf - mu) * lax.rsqrt(var + LN_EPS) * w.astype(jnp.float32) + b.astype(jnp.float32)


def hgrn2_lower_bound(lb_logits, layer):
    p = jax.nn.softmax(lb_logits.astype(jnp.float32), axis=0)
    return jnp.cumsum(p, axis=0)[layer]


def hgrn2_recurrence(q, f_logit, i, g, lb, norm_w):
    f32 = jnp.float32
    bsz, seq, _ = q.shape
    zf = f_logit.astype(f32)
    lb = lb.astype(f32)
    log_f = jnp.log(lb + (1.0 - lb) * jax.nn.sigmoid(zf))
    k = (1.0 - lb) * jax.nn.sigmoid(-zf)
    n_chunks = seq // A_CHUNK

    def to_chunks(t, d):
        t = t.astype(f32).reshape(bsz, n_chunks, A_CHUNK, A_HEADS, d)
        return t.transpose(1, 0, 3, 2, 4)

    qc, kc, lfc = to_chunks(q, A_DK), to_chunks(k, A_DK), to_chunks(log_f, A_DK)
    vc = to_chunks(i, A_DV)
    causal = jnp.tril(jnp.ones((A_CHUNK, A_CHUNK), dtype=bool))[:, :, None]

    def step(state, inp):
        qb, kb, vb, lfb = inp
        cum = jnp.cumsum(lfb, axis=2)
        diff = cum[:, :, :, None, :] - cum[:, :, None, :, :]
        decay = jnp.exp(jnp.where(causal, diff, -jnp.inf))
        scores = jnp.einsum('bhtk,bhsk,bhtsk->bhts', qb, kb, decay)
        out = (jnp.einsum('bhts,bhsv->bhtv', scores, vb)
               + jnp.einsum('bhtk,bhkv->bhtv', qb * jnp.exp(cum), state))
        last = cum[:, :, -1:, :]
        new_state = (jnp.exp(last[:, :, 0, :, None]) * state
                     + jnp.einsum('bhsk,bhsv->bhkv', kb * jnp.exp(last - cum), vb))
        return new_state, out

    state0 = jnp.zeros((bsz, A_HEADS, A_DK, A_DV), f32)
    _, o = lax.scan(step, state0, (qc, kc, vc, lfc))
    o = o.transpose(1, 0, 3, 2, 4).reshape(bsz, seq, A_HEADS, A_DV)
    o = o * lax.rsqrt(jnp.mean(jnp.square(o), axis=-1, keepdims=True) + RMS_EPS) * norm_w.astype(f32)
    gate = jax.nn.silu(g.astype(f32)).reshape(bsz, seq, A_HEADS, A_DV)
    return (o * gate).reshape(bsz, seq, A_WIDTH)


def multiscale_pool(v, w_pool, pool_scale):
    f32 = jnp.float32
    bsz, seq, _ = v.shape
    vg = v.astype(f32).reshape(bsz, seq, B_GROUPS, B_GROUP_DIM)
    pos = jnp.arange(1, seq + 1, dtype=f32)
    outs = []
    for gidx, win in enumerate(B_WINDOWS):
        xg = vg[:, :, gidx, :]
        cs = jnp.cumsum(jnp.pad(xg, ((0, 0), (win, 0), (0, 0))), axis=1)
        window_sum = cs[:, win:] - cs[:, :seq]
        count = jnp.minimum(pos, float(win))[None, :, None]
        outs.append(window_sum / count - xg)
    pooled = jnp.stack(outs, axis=2)
    mixed = jnp.einsum('bsgc,gcd->bsgd', pooled, w_pool.astype(f32)).reshape(bsz, seq, B_WIDTH)
    return mixed * pool_scale.astype(f32)


def recurrent_pool_mixer(x, lb_logits, layer, w_in, a_norm_w, w_pool, pool_scale, w_o):
    proj = x @ w_in
    q_a, f_a, i_a, g_a, v_b = jnp.split(proj, EVEN_SPLITS, axis=-1)
    lb = hgrn2_lower_bound(lb_logits, layer)
    a_out = hgrn2_recurrence(q_a, f_a, i_a, g_a, lb, a_norm_w)
    b_out = multiscale_pool(v_b, w_pool, pool_scale)
    return jnp.concatenate([a_out, b_out], axis=-1) @ w_o


def split_heads(t, n_heads):
    bsz, seq, width = t.shape
    return t.reshape(bsz, seq, n_heads, width // n_heads).transpose(0, 2, 1, 3)


def stick_breaking_attention(q, k, v):
    f32 = jnp.float32
    seq = q.shape[2]
    scale = C_HEAD_DIM ** -0.5
    outs = []
    for blk in range(seq // C_QBLOCK):
        q0 = blk * C_QBLOCK
        kv_len = q0 + C_QBLOCK
        qb = q[:, :, q0:kv_len]
        kb = k[:, :, :kv_len]
        vb = v[:, :, :kv_len].astype(f32)
        z = jnp.einsum('bhqd,bhkd->bhqk', qb, kb).astype(f32) * scale
        q_pos = q0 + jnp.arange(C_QBLOCK)
        k_pos = jnp.arange(kv_len)
        strict = k_pos[None, :] < q_pos[:, None]
        log_beta = jax.nn.log_sigmoid(z)
        log_one_minus = jnp.where(strict, log_beta - z, 0.0)
        later = lax.cumsum(log_one_minus, axis=3, reverse=True) - log_one_minus
        weights = jnp.where(strict, jnp.exp(log_beta + later), 0.0)
        outs.append(jnp.einsum('bhqk,bhkd->bhqd', weights, vb))
    return jnp.concatenate(outs, axis=2)


def stick_breaking_mixer(x, w_in, w_o):
    bsz, seq, _ = x.shape
    proj = x @ w_in
    q, k, v = jnp.split(proj, 3, axis=-1)
    o = stick_breaking_attention(split_heads(q, C_HEADS), split_heads(k, C_HEADS), split_heads(v, C_HEADS))
    o = o.transpose(0, 2, 1, 3).reshape(bsz, seq, C_WIDTH)
    return o @ w_o


def clamped_swiglu(h):
    x_glu = jnp.minimum(h[..., ::2], SWIGLU_LIMIT)
    x_lin = jnp.clip(h[..., 1::2], -SWIGLU_LIMIT, SWIGLU_LIMIT)
    return x_glu * jax.nn.sigmoid(SWIGLU_ALPHA * x_glu) * (x_lin + 1.0)


def moe_ffn(x, w_router, b_router, w1, b1, w2, b2):
    f32 = jnp.float32
    bsz, seq, d = x.shape
    xf = x.reshape(bsz * seq, d)
    logits = (xf @ w_router + b_router).astype(f32)
    top_vals, top_idx = lax.top_k(logits, TOP_K)
    gates = jax.nn.softmax(top_vals, axis=-1)
    combine = jnp.einsum('tk,tke->te', gates, jax.nn.one_hot(top_idx, N_EXPERTS, dtype=f32))
    y = jnp.zeros((bsz * seq, d), f32)
    for e in range(N_EXPERTS):
        h = clamped_swiglu(xf @ w1[e] + b1[e])
        y = y + combine[:, e:e + 1] * (h @ w2[e] + b2[e])
    return y.reshape(bsz, seq, d)


def setup_inputs(seed: int = 0) -> dict:
    key = jax.random.key(seed)
    keys = iter(jax.random.split(key, 40))
    d = D_MODEL

    def nrm(shape, scale):
        return jax.random.normal(next(keys), shape, jnp.float32) * scale

    inp = {}
    inp['x'] = nrm((BATCH, SEQ, d), 1.0)
    inp['hgrn_lb_logits'] = nrm((DEPTH + 1, A_FDIM), 0.5)

    def add_ln(prefix, name):
        inp[prefix + name + '_w'] = 1.0 + nrm((d,), 0.05)
        inp[prefix + name + '_b'] = nrm((d,), 0.02)

    def add_moe(prefix):
        inp[prefix + 'w_router'] = nrm((d, N_EXPERTS), d ** -0.5)
        inp[prefix + 'b_router'] = nrm((N_EXPERTS,), 0.01)
        inp[prefix + 'w1'] = nrm((N_EXPERTS, d, 2 * D_FF), d ** -0.5)
        inp[prefix + 'b1'] = nrm((N_EXPERTS, 2 * D_FF), 0.02)
        inp[prefix + 'w2'] = nrm((N_EXPERTS, D_FF, d), D_FF ** -0.5 * DEEPNORM_BETA)
        inp[prefix + 'b2'] = nrm((N_EXPERTS, d), 0.02)

    for layer in range(DEPTH):
        p = 'l%d_' % layer
        if layer % 2 == 0:
            inp[p + 'w_in'] = nrm((d, EVEN_IN_COLS), d ** -0.5)
            inp[p + 'a_norm_w'] = 1.0 + nrm((A_DV,), 0.05)
            inp[p + 'w_pool'] = nrm((B_GROUPS, B_GROUP_DIM, B_GROUP_DIM), B_GROUP_DIM ** -0.5)
            inp[p + 'pool_scale'] = 1.0 + nrm((B_WIDTH,), 0.1)
            inp[p + 'w_o'] = nrm((EVEN_MIX_WIDTH, d), EVEN_MIX_WIDTH ** -0.5 * DEEPNORM_BETA)
        else:
            inp[p + 'w_in'] = nrm((d, ODD_IN_COLS), d ** -0.5)
            inp[p + 'w_o'] = nrm((C_WIDTH, d), C_WIDTH ** -0.5 * DEEPNORM_BETA)
        add_ln(p, 'ln1')
        add_moe(p)
        add_ln(p, 'ln2')
    return inp


def reference(x, hgrn_lb_logits,
              l0_w_in, l0_a_norm_w, l0_w_pool, l0_pool_scale, l0_w_o, l0_ln1_w, l0_ln1_b,
              l0_w_router, l0_b_router, l0_w1, l0_b1, l0_w2, l0_b2, l0_ln2_w, l0_ln2_b,
              l1_w_in, l1_w_o, l1_ln1_w, l1_ln1_b,
              l1_w_router, l1_b_router, l1_w1, l1_b1, l1_w2, l1_b2, l1_ln2_w, l1_ln2_b):
    out_dtype = x.dtype
    mixer_params = ((l0_w_in, l0_a_norm_w, l0_w_pool, l0_pool_scale, l0_w_o),
                    (l1_w_in, l1_w_o))
    ln1_params = ((l0_ln1_w, l0_ln1_b), (l1_ln1_w, l1_ln1_b))
    moe_params = ((l0_w_router, l0_b_router, l0_w1, l0_b1, l0_w2, l0_b2),
                  (l1_w_router, l1_b_router, l1_w1, l1_b1, l1_w2, l1_b2))
    ln2_params = ((l0_ln2_w, l0_ln2_b), (l1_ln2_w, l1_ln2_b))
    for layer in range(DEPTH):
        if layer % 2 == 0:
            h = recurrent_pool_mixer(x, hgrn_lb_logits, layer, *mixer_params[layer])
        else:
            h = stick_breaking_mixer(x, *mixer_params[layer])
        x = layer_norm(DEEPNORM_ALPHA * x.astype(jnp.float32) + h, *ln1_params[layer]).astype(out_dtype)
        h = moe_ffn(x, *moe_params[layer])
        x = layer_norm(DEEPNORM_ALPHA * x.astype(jnp.float32) + h, *ln2_params[layer]).astype(out_dtype)
    return x
```

```python
import functools

import jax
import jax.numpy as jnp
from jax import lax
from jax.experimental import pallas as pl
from jax.experimental.pallas import tpu as pltpu

F32 = jnp.float32
BF16 = jnp.bfloat16
I32 = jnp.int32

DEPTH = 2
DEEPNORM_ALPHA = (2.0 * DEPTH) ** 0.25
LN_EPS = 1e-5
RMS_EPS = 1e-6

A_HEADS = 4
A_DK = 128
POOL_WINDOWS = (2, 4, 8, 16)
POOL_HALO = 16
C_HEAD_DIM = 64
N_EXPERTS = 32
TOP_K = 4
SWIGLU_ALPHA = 1.702
SWIGLU_LIMIT = 7.0

HGRN_CHUNK = 16
VMEM_LIMIT = 48 * 1024 * 1024


def _params(*sem):
    return pltpu.CompilerParams(dimension_semantics=sem, vmem_limit_bytes=VMEM_LIMIT)


def _dot(a, b):
    return jnp.dot(a, b, preferred_element_type=F32)


def _dot_nt(a, b):
    return lax.dot_general(a, b, (((1,), (1,)), ((), ())), preferred_element_type=F32)


def _dot_tn(a, b):
    return lax.dot_general(a, b, (((0,), (0,)), ((), ())), preferred_element_type=F32)


def _sigmoid(x):
    e = jnp.exp(-jnp.abs(x))
    r = 1.0 / (1.0 + e)
    return jnp.where(x >= 0, r, e * r)


def _layer_norm(y, w, b):
    mu = jnp.mean(y, axis=-1, keepdims=True)
    yc = y - mu
    var = jnp.mean(yc * yc, axis=-1, keepdims=True)
    return yc * lax.rsqrt(var + LN_EPS) * w + b


def _proj_kernel(x_ref, w_ref, o_ref):
    o_ref[...] = _dot(x_ref[...].astype(BF16), w_ref[...]).astype(o_ref.dtype)


def _proj(x2d, w, out_dtype, tm=512):
    m, k = x2d.shape
    n = w.shape[1]
    tm = min(tm, m)
    return pl.pallas_call(
        _proj_kernel,
        grid=(m // tm,),
        in_specs=[pl.BlockSpec((tm, k), lambda i: (i, 0)),
                  pl.BlockSpec((k, n), lambda i: (0, 0))],
        out_specs=pl.BlockSpec((tm, n), lambda i: (i, 0)),
        out_shape=jax.ShapeDtypeStruct((m, n), out_dtype),
        compiler_params=_params("arbitrary"),
        name="proj",
    )(x2d, w)


def _hgrn_kernel(q_ref, f_ref, i_ref, g_ref, lb_ref, nw_ref, ltri_ref, o_ref,
                 st_ref, cum_ref, kk_ref, oacc_ref, *, ts):
    c = HGRN_CHUNK

    @pl.when(pl.program_id(2) == 0)
    def _():
        st_ref[...] = jnp.zeros_like(st_ref)

    z = f_ref[...]
    lb = lb_ref[...]
    e = jnp.exp(-jnp.abs(z))
    r = 1.0 / (1.0 + e)
    er = e * r
    sig = jnp.where(z >= 0, r, er)
    sig_neg = jnp.where(z >= 0, er, r)
    logf = jnp.log(lb + (1.0 - lb) * sig)
    kk_ref[...] = (1.0 - lb) * sig_neg

    ltri = ltri_ref[...]
    x1 = logf.astype(BF16)
    r1 = logf - x1.astype(F32)
    x2 = r1.astype(BF16)
    x3 = (r1 - x2.astype(F32)).astype(BF16)
    cum_ref[...] = _dot(ltri, x1) + _dot(ltri, x2) + _dot(ltri, x3)

    t_iota = lax.broadcasted_iota(I32, (c, A_DK), 0)

    def chunk(ci, carry):
        r0 = pl.multiple_of(ci * c, c)
        cum = cum_ref[pl.ds(r0, c), :]
        q = q_ref[pl.ds(r0, c), :]
        k = kk_ref[pl.ds(r0, c), :]
        v = i_ref[pl.ds(r0, c), :]
        last = cum[c - 1:c, :]
        st = st_ref[...]
        qd = (q * jnp.exp(cum)).astype(BF16)
        o = _dot_nt(qd, st.astype(BF16))
        for s in range(c):
            d = jnp.where(t_iota >= s, cum - cum[s:s + 1, :], -1e30)
            p = q * (k[s:s + 1, :] * jnp.exp(d))
            o = o + jnp.sum(p, axis=-1, keepdims=True) * v[s:s + 1, :]
        oacc_ref[pl.ds(r0, c), :] = o
        kd = (k * jnp.exp(last - cum)).astype(BF16)
        st_ref[...] = jnp.exp(last) * st + _dot_tn(v.astype(BF16), kd)
        return carry

    lax.fori_loop(0, ts // c, chunk, 0)

    o = oacc_ref[...]
    o = o * lax.rsqrt(jnp.mean(o * o, axis=-1, keepdims=True) + RMS_EPS) * nw_ref[...]
    g = g_ref[...]
    o_ref[...] = o * (g * _sigmoid(g))


def _hgrn(proj, lb, norm_w, bsz, seq, ts=256):
    ts = min(ts, seq)
    ns = seq // ts
    h = A_HEADS
    blk = jnp.arange(ts) // HGRN_CHUNK
    ltri = ((blk[:, None] == blk[None, :]) &
            (jnp.arange(ts)[:, None] >= jnp.arange(ts)[None, :])).astype(BF16)

    def col(off):
        return pl.BlockSpec((ts, A_DK), lambda b, hh, s, off=off: (b * ns + s, off + hh))

    return pl.pallas_call(
        functools.partial(_hgrn_kernel, ts=ts),
        grid=(bsz, h, ns),
        in_specs=[col(0), col(h), col(2 * h), col(3 * h),
                  pl.BlockSpec((1, A_DK), lambda b, hh, s: (0, hh)),
                  pl.BlockSpec((1, A_DK), lambda b, hh, s: (0, 0)),
                  pl.BlockSpec((ts, ts), lambda b, hh, s: (0, 0))],
        out_specs=pl.BlockSpec((ts, A_DK), lambda b, hh, s: (b * ns + s, hh)),
        out_shape=jax.ShapeDtypeStruct((bsz * seq, h * A_DK), F32),
        scratch_shapes=[pltpu.VMEM((A_DK, A_DK), F32),
                        pltpu.VMEM((ts, A_DK), F32),
                        pltpu.VMEM((ts, A_DK), F32),
                        pltpu.VMEM((ts, A_DK), F32)],
        compiler_params=_params("arbitrary", "arbitrary", "arbitrary"),
        name="hgrn2",
    )(proj, proj, proj, proj, lb, norm_w, ltri)


def _pool_kernel(v_ref, wp_ref, sc_ref, o_ref, buf_ref, *, ts):
    s = pl.program_id(1)
    hl = POOL_HALO

    @pl.when(s == 0)
    def _():
        buf_ref[0:hl, :] = jnp.zeros((hl, buf_ref.shape[1]), F32)

    x = v_ref[...]
    buf_ref[hl:hl + ts, :] = x
    pos = (s * ts + 1 + lax.broadcasted_iota(I32, (ts, 1), 0)).astype(F32)
    for g, win in enumerate(POOL_WINDOWS):
        lo, hi = g * 128, (g + 1) * 128
        xg = x[:, lo:hi]
        acc = xg
        for j in range(1, win):
            acc = acc + buf_ref[hl - j:hl - j + ts, lo:hi]
        pooled = acc / jnp.minimum(pos, float(win)) - xg
        o_ref[:, lo:hi] = _dot(pooled.astype(BF16), wp_ref[g]) * sc_ref[:, lo:hi]
    buf_ref[0:hl, :] = x[ts - hl:ts, :]


def _pool(proj, w_pool, pool_scale, bsz, seq, col_block, ts=512):
    ts = min(ts, seq)
    ns = seq // ts
    width = w_pool.shape[0] * w_pool.shape[1]
    return pl.pallas_call(
        functools.partial(_pool_kernel, ts=ts),
        grid=(bsz, ns),
        in_specs=[pl.BlockSpec((ts, width), lambda b, s: (b * ns + s, col_block)),
                  pl.BlockSpec(w_pool.shape, lambda b, s: (0, 0, 0)),
                  pl.BlockSpec((1, width), lambda b, s: (0, 0))],
        out_specs=pl.BlockSpec((ts, width), lambda b, s: (b * ns + s, 0)),
        out_shape=jax.ShapeDtypeStruct((bsz * seq, width), F32),
        scratch_shapes=[pltpu.VMEM((POOL_HALO + ts, width), F32)],
        compiler_params=_params("arbitrary", "arbitrary"),
        name="pool",
    )(proj, w_pool, pool_scale)


def _mix_router_kernel(*refs, n_parts, tm):
    parts = refs[:n_parts]
    wos = refs[n_parts:2 * n_parts]
    x_ref, lnw_ref, lnb_ref, wr_ref, br_ref, tri_ref = refs[2 * n_parts:2 * n_parts + 6]
    x1_ref, idx_ref, gate_ref, rank_ref, cnt_ref, run_ref = refs[2 * n_parts + 6:]

    @pl.when(pl.program_id(0) == 0)
    def _():
        run_ref[...] = jnp.zeros_like(run_ref)

    h = _dot(parts[0][...].astype(BF16), wos[0][...])
    for p_ref, w_ref in zip(parts[1:], wos[1:]):
        h = h + _dot(p_ref[...].astype(BF16), w_ref[...])
    x1 = _layer_norm(DEEPNORM_ALPHA * x_ref[...] + h, lnw_ref[...], lnb_ref[...])
    x1_ref[...] = x1

    logits = jnp.dot(x1, wr_ref[...], preferred_element_type=F32,
                     precision=lax.Precision.HIGHEST) + br_ref[...]
    ne = logits.shape[1]
    e_iota = lax.broadcasted_iota(I32, (tm, ne), 1)
    k_iota = lax.broadcasted_iota(I32, (tm, TOP_K), 1)
    l = logits
    vals, idxs = [], []
    for _ in range(TOP_K):
        m = jnp.max(l, axis=-1, keepdims=True)
        ik = jnp.min(jnp.where(l == m, e_iota, ne), axis=-1, keepdims=True)
        vals.append(m)
        idxs.append(ik)
        l = jnp.where(e_iota == ik, -jnp.inf, l)
    exps = [jnp.exp(v - vals[0]) for v in vals]
    tot = exps[0] + exps[1] + exps[2] + exps[3]

    onehot = jnp.zeros((tm, ne), F32)
    for ik in idxs:
        onehot = onehot + (e_iota == ik).astype(F32)
    base = _dot(tri_ref[...], onehot.astype(BF16)) + run_ref[...]

    idx_out = jnp.zeros((tm, TOP_K), I32)
    gate_out = jnp.zeros((tm, TOP_K), F32)
    rank_out = jnp.zeros((tm, TOP_K), I32)
    for kk in range(TOP_K):
        rk = jnp.sum(jnp.where(e_iota == idxs[kk], base, 0.0), axis=-1, keepdims=True)
        idx_out = jnp.where(k_iota == kk, idxs[kk], idx_out)
        gate_out = jnp.where(k_iota == kk, exps[kk] / tot, gate_out)
        rank_out = jnp.where(k_iota == kk, rk.astype(I32), rank_out)
    idx_ref[...] = idx_out
    gate_ref[...] = gate_out
    rank_ref[...] = rank_out
    run_ref[...] = run_ref[...] + jnp.sum(onehot, axis=0, keepdims=True)
    cnt_ref[...] = run_ref[...].astype(I32)


def _mix_router(parts, w_o, x2d, ln_w, ln_b, w_router, b_router, tm=256):
    t, d = x2d.shape
    tm = min(tm, t)
    n_parts = len(parts)
    wos, off = [], 0
    for p in parts:
        wos.append(lax.slice_in_dim(w_o, off, off + p.shape[1], axis=0))
        off += p.shape[1]
    tri = (jnp.arange(tm)[:, None] > jnp.arange(tm)[None, :]).astype(BF16)
    ne = w_router.shape[1]
    row = lambda i: (i, 0)
    fixed = lambda i: (0, 0)
    in_specs = ([pl.BlockSpec((tm, p.shape[1]), row) for p in parts] +
                [pl.BlockSpec(w.shape, fixed) for w in wos] +
                [pl.BlockSpec((tm, d), row),
                 pl.BlockSpec((1, d), fixed), pl.BlockSpec((1, d), fixed),
                 pl.BlockSpec((d, ne), fixed), pl.BlockSpec((1, ne), fixed),
                 pl.BlockSpec((tm, tm), fixed)])
    out_specs = [pl.BlockSpec((tm, d), row),
                 pl.BlockSpec((tm, TOP_K), row),
                 pl.BlockSpec((tm, TOP_K), row),
                 pl.BlockSpec((tm, TOP_K), row),
                 pl.BlockSpec((1, ne), fixed)]
    out_shape = [jax.ShapeDtypeStruct((t, d), F32),
                 jax.ShapeDtypeStruct((t, TOP_K), I32),
                 jax.ShapeDtypeStruct((t, TOP_K), F32),
                 jax.ShapeDtypeStruct((t, TOP_K), I32),
                 jax.ShapeDtypeStruct((1, ne), I32)]
    return pl.pallas_call(
        functools.partial(_mix_router_kernel, n_parts=n_parts, tm=tm),
        grid=(t // tm,),
        in_specs=in_specs,
        out_specs=out_specs,
        out_shape=out_shape,
        scratch_shapes=[pltpu.VMEM((1, ne), F32)],
        compiler_params=_params("arbitrary"),
        name="mix_router",
    )(*parts, *wos, x2d, ln_w, ln_b, w_router, b_router, tri)


def _dispatch_kernel(starts_ref, pc_ref, pos_ref, x_ref, xs_ref, zero_ref, sem, zsem,
                     *, tm, te):
    ne = starts_ref.shape[0]
    n_tiles = xs_ref.shape[0] // te

    def zero_tile(tile):
        return pltpu.make_async_copy(
            zero_ref, xs_ref.at[pl.ds(pl.multiple_of(tile * te, te), te)], zsem)

    def last_tile(e):
        return (starts_ref[e] + pc_ref[e]) // te - 1

    @pl.when(pl.program_id(0) == 0)
    def _():
        zero_ref[...] = jnp.zeros_like(zero_ref)
        n_used = (starts_ref[ne - 1] + pc_ref[ne - 1]) // te

        def start(e, carry):
            @pl.when(pc_ref[e] > 0)
            def _():
                zero_tile(last_tile(e)).start()
            return carry

        def wait(e, carry):
            @pl.when(pc_ref[e] > 0)
            def _():
                zero_tile(last_tile(e)).wait()
            return carry

        def start_tail(tile, carry):
            zero_tile(tile).start()
            return carry

        def wait_tail(tile, carry):
            zero_tile(tile).wait()
            return carry

        lax.fori_loop(0, ne, start, 0)
        lax.fori_loop(n_used, n_tiles, start_tail, 0)
        lax.fori_loop(0, ne, wait, 0)
        lax.fori_loop(n_used, n_tiles, wait_tail, 0)

    def row(r, carry):
        for kk in range(TOP_K):
            p = pos_ref[0, 0, r * TOP_K + kk]
            pltpu.make_async_copy(x_ref.at[pl.ds(r, 1)], xs_ref.at[pl.ds(p, 1)], sem).start()
        return carry

    lax.fori_loop(0, tm, row, 0)
    for kk in range(TOP_K):
        pltpu.make_async_copy(x_ref, xs_ref.at[pl.ds(0, tm)], sem).wait()


def _dispatch(x1, pos_tiles, starts, pc, n_slots, tm, te):
    t, d = x1.shape
    grid_spec = pltpu.PrefetchScalarGridSpec(
        num_scalar_prefetch=2,
        grid=(t // tm,),
        in_specs=[pl.BlockSpec((1, 1, tm * TOP_K), lambda i, *_: (i, 0, 0),
                               memory_space=pltpu.SMEM),
                  pl.BlockSpec((tm, d), lambda i, *_: (i, 0))],
        out_specs=pl.BlockSpec(memory_space=pl.ANY),
        scratch_shapes=[pltpu.VMEM((te, d), F32),
                        pltpu.SemaphoreType.DMA,
                        pltpu.SemaphoreType.DMA],
    )
    return pl.pallas_call(
        functools.partial(_dispatch_kernel, tm=tm, te=te),
        grid_spec=grid_spec,
        out_shape=jax.ShapeDtypeStruct((n_slots, d), F32),
        compiler_params=_params("arbitrary"),
        name="moe_dispatch",
    )(starts, pc, pos_tiles, x1)


def _expert_kernel(te_ref, nu_ref, x_ref, w1g_ref, w1l_ref, b1g_ref, b1l_ref, w2_ref, b2_ref,
                   o_ref):
    @pl.when(pl.program_id(0) < nu_ref[0])
    def _():
        xb = x_ref[...].astype(BF16)
        hg = _dot(xb, w1g_ref[0]) + b1g_ref[0]
        hl = _dot(xb, w1l_ref[0]) + b1l_ref[0]
        glu = jnp.minimum(hg, SWIGLU_LIMIT)
        lin = jnp.clip(hl, -SWIGLU_LIMIT, SWIGLU_LIMIT)
        act = glu * _sigmoid(SWIGLU_ALPHA * glu) * (lin + 1.0)
        o_ref[...] = _dot(act.astype(BF16), w2_ref[0]) + b2_ref[0]

    @pl.when(pl.program_id(0) >= nu_ref[0])
    def _():
        o_ref[...] = jnp.zeros_like(o_ref)


def _experts(xs, tile_expert, n_used, w1g, w1l, b1g, b1l, w2, b2, te):
    n_slots, d = xs.shape
    dff = w1g.shape[2]
    n_tiles = n_slots // te

    def row(i, te_ref, nu_ref):
        return (jnp.minimum(i, nu_ref[0] - 1), 0)

    def wsel(i, te_ref, nu_ref):
        return (te_ref[i], 0, 0)

    grid_spec = pltpu.PrefetchScalarGridSpec(
        num_scalar_prefetch=2,
        grid=(n_tiles,),
        in_specs=[pl.BlockSpec((te, d), row),
                  pl.BlockSpec((1, d, dff), wsel), pl.BlockSpec((1, d, dff), wsel),
                  pl.BlockSpec((1, 1, dff), wsel), pl.BlockSpec((1, 1, dff), wsel),
                  pl.BlockSpec((1, dff, d), wsel), pl.BlockSpec((1, 1, d), wsel)],
        out_specs=pl.BlockSpec((te, d), lambda i, te_ref, nu_ref: (i, 0)),
    )
    return pl.pallas_call(
        _expert_kernel,
        grid_spec=grid_spec,
        out_shape=jax.ShapeDtypeStruct((n_slots, d), F32),
        compiler_params=_params("arbitrary"),
        name="moe_experts",
    )(tile_expert, n_used, xs, w1g, w1l, b1g, b1l, w2, b2)


def _combine_kernel(pos_ref, x_ref, gate_ref, lnw_ref, lnb_ref, eo_ref, o_ref, buf_ref, sem,
                    *, tm):
    def row(r, carry):
        for kk in range(TOP_K):
            p = pos_ref[0, 0, r * TOP_K + kk]
            pltpu.make_async_copy(eo_ref.at[pl.ds(p, 1)], buf_ref.at[kk, pl.ds(r, 1)], sem).start()
        return carry

    lax.fori_loop(0, tm, row, 0)
    for kk in range(TOP_K):
        pltpu.make_async_copy(eo_ref.at[pl.ds(0, tm)], buf_ref.at[kk], sem).wait()

    g = gate_ref[...]
    y = g[:, 0:1] * buf_ref[0]
    for kk in range(1, TOP_K):
        y = y + g[:, kk:kk + 1] * buf_ref[kk]
    o_ref[...] = _layer_norm(DEEPNORM_ALPHA * x_ref[...] + y, lnw_ref[...], lnb_ref[...])


def _combine(eo, pos_tiles, x1, gates, ln_w, ln_b, tm):
    t, d = x1.shape
    row = lambda i: (i, 0)
    fixed = lambda i: (0, 0)
    return pl.pallas_call(
        functools.partial(_combine_kernel, tm=tm),
        grid=(t // tm,),
        in_specs=[pl.BlockSpec((1, 1, tm * TOP_K), lambda i: (i, 0, 0), memory_space=pltpu.SMEM),
                  pl.BlockSpec((tm, d), row),
                  pl.BlockSpec((tm, TOP_K), row),
                  pl.BlockSpec((1, d), fixed), pl.BlockSpec((1, d), fixed),
                  pl.BlockSpec(memory_space=pl.ANY)],
        out_specs=pl.BlockSpec((tm, d), row),
        out_shape=jax.ShapeDtypeStruct((t, d), F32),
        scratch_shapes=[pltpu.VMEM((TOP_K, tm, d), F32), pltpu.SemaphoreType.DMA],
        compiler_params=_params("arbitrary"),
        name="moe_combine",
    )(pos_tiles, x1, gates, ln_w, ln_b, eo)


def _moe(x1, idx, gates, rank, counts, w1, b1, w2, b2, ln_w, ln_b, tm=256, te=256):
    t, d = x1.shape
    tm = min(tm, t)
    ne = w1.shape[0]
    counts = counts[0]
    pc = ((counts + te - 1) // te) * te
    ends = jnp.cumsum(pc)
    starts = ends - pc
    n_tiles = (t * TOP_K) // te + ne
    n_used = (ends[-1] // te).astype(I32)
    tile_ids = jnp.arange(n_tiles, dtype=I32)
    te_raw = jnp.minimum(jnp.searchsorted(ends // te, tile_ids, side="right"), ne - 1).astype(I32)
    tile_expert = jnp.where(tile_ids < n_used, te_raw, jnp.take(te_raw, n_used - 1))
    pos = jnp.take(starts, idx) + rank
    pos_tiles = pos.reshape(t // tm, 1, tm * TOP_K)

    w1g = w1[:, :, 0::2].astype(BF16)
    w1l = w1[:, :, 1::2].astype(BF16)
    b1g = b1[:, None, 0::2]
    b1l = b1[:, None, 1::2]

    xs = _dispatch(x1, pos_tiles, starts.astype(I32), pc.astype(I32), n_tiles * te, tm, te)
    eo = _experts(xs, tile_expert, n_used.reshape(1), w1g, w1l, b1g, b1l,
                  w2.astype(BF16), b2[:, None, :], te)
    return _combine(eo, pos_tiles, x1, gates, ln_w, ln_b, tm)


def _attn_kernel(q_ref, k_ref, v_ref, u_ref, o_ref, *, tq):
    i = pl.program_id(2)
    hd = C_HEAD_DIM
    lane = lax.broadcasted_iota(I32, (tq, 2 * hd), 1)
    q = q_ref[...]
    qs = (jnp.where(lane < hd, q, jnp.zeros_like(q)), jnp.where(lane >= hd, q, jnp.zeros_like(q)))
    u = u_ref[...]
    strict = (lax.broadcasted_iota(I32, (tq, tq), 1) < lax.broadcasted_iota(I32, (tq, tq), 0))

    def block(j, qh, carry, acc, diag):
        r0 = pl.multiple_of(j * tq, tq)
        kj = k_ref[pl.ds(r0, tq), :]
        vj = v_ref[pl.ds(r0, tq), :]
        z = _dot_nt(qh, kj)
        l1m = -(jnp.maximum(z, 0.0) + jnp.log(1.0 + jnp.exp(-jnp.abs(z))))
        if diag:
            l1m = jnp.where(strict, l1m, 0.0)
        suf = _dot(l1m.astype(BF16), u)
        w = jnp.exp(z + suf + carry)
        if diag:
            w = jnp.where(strict, w, 0.0)
        acc = acc + _dot(w.astype(BF16), vj)
        carry = carry + suf[:, 0:1]
        return carry, acc

    zero_c = jnp.zeros((tq, 1), F32)
    zero_a = jnp.zeros((tq, 2 * hd), F32)
    ca, aa = block(i, qs[0], zero_c, zero_a, True)
    cb, ab = block(i, qs[1], zero_c, zero_a, True)

    def body(jj, st):
        ca, aa, cb, ab = st
        j = i - 1 - jj
        ca, aa = block(j, qs[0], ca, aa, False)
        cb, ab = block(j, qs[1], cb, ab, False)
        return ca, aa, cb, ab

    ca, aa, cb, ab = lax.fori_loop(0, i, body, (ca, aa, cb, ab))
    o_ref[...] = jnp.where(lane < hd, aa, ab).astype(o_ref.dtype)


def _attention(qkv, bsz, seq, width, tq=256):
    tq = min(tq, seq)
    npairs = width // (2 * C_HEAD_DIM)
    nq = seq // tq
    u = (jnp.arange(tq)[:, None] >= jnp.arange(tq)[None, :]).astype(BF16)
    qkv3 = qkv.reshape(bsz, seq, 3 * width)
    blk = 2 * C_HEAD_DIM
    return pl.pallas_call(
        functools.partial(_attn_kernel, tq=tq),
        grid=(bsz, npairs, nq),
        in_specs=[pl.BlockSpec((None, tq, blk), lambda b, p, i: (b, i, p)),
                  pl.BlockSpec((None, seq, blk), lambda b, p, i: (b, 0, npairs + p)),
                  pl.BlockSpec((None, seq, blk), lambda b, p, i: (b, 0, 2 * npairs + p)),
                  pl.BlockSpec((tq, tq), lambda b, p, i: (0, 0))],
        out_specs=pl.BlockSpec((None, tq, blk), lambda b, p, i: (b, i, p)),
        out_shape=jax.ShapeDtypeStruct((bsz, seq, width), BF16),
        compiler_params=_params("arbitrary", "arbitrary", "arbitrary"),
        name="stickbreak_attn",
    )(qkv3, qkv3, qkv3, u).reshape(bsz * seq, width)


def kernel(x, hgrn_lb_logits, l0_w_in, l0_a_norm_w, l0_w_pool, l0_pool_scale, l0_w_o, l0_ln1_w, l0_ln1_b, l0_w_router, l0_b_router, l0_w1, l0_b1, l0_w2, l0_b2, l0_ln2_w, l0_ln2_b, l1_w_in, l1_w_o, l1_ln1_w, l1_ln1_b, l1_w_router, l1_b_router, l1_w1, l1_b1, l1_w2, l1_b2, l1_ln2_w, l1_ln2_b):
    bsz, seq, d = x.shape
    out_dtype = x.dtype
    x2d = x.reshape(bsz * seq, d).astype(F32)
    r1 = lambda a: a.reshape(1, -1).astype(F32)

    a_width = A_HEADS * A_DK
    lb = jnp.cumsum(jax.nn.softmax(hgrn_lb_logits.astype(F32), axis=0), axis=0)[0]
    proj0 = _proj(x2d, l0_w_in.astype(BF16), F32)
    a_out = _hgrn(proj0, r1(lb), r1(l0_a_norm_w), bsz, seq)
    b_width = l0_w_pool.shape[0] * l0_w_pool.shape[1]
    b_out = _pool(proj0, l0_w_pool.astype(BF16), r1(l0_pool_scale), bsz, seq,
                  col_block=(4 * a_width) // b_width)
    x1, idx, gates, rank, counts = _mix_router(
        [a_out, b_out], l0_w_o.astype(BF16), x2d, r1(l0_ln1_w), r1(l0_ln1_b),
        l0_w_router.astype(F32), r1(l0_b_router))
    x2d = _moe(x1, idx, gates, rank, counts, l0_w1, l0_b1, l0_w2, l0_b2,
               r1(l0_ln2_w), r1(l0_ln2_b))

    c_width = l1_w_in.shape[1] // 3
    qscale = jnp.concatenate([jnp.full((c_width,), C_HEAD_DIM ** -0.5, F32),
                              jnp.ones((2 * c_width,), F32)])
    qkv = _proj(x2d, (l1_w_in * qscale).astype(BF16), BF16)
    o = _attention(qkv, bsz, seq, c_width)
    x1, idx, gates, rank, counts = _mix_router(
        [o], l1_w_o.astype(BF16), x2d, r1(l1_ln1_w), r1(l1_ln1_b),
        l1_w_router.astype(F32), r1(l1_b_router))
    x2d = _moe(x1, idx, gates, rank, counts, l1_w1, l1_b1, l1_w2, l1_b2,
               r1(l1_ln2_w), r1(l1_ln2_b))
    return x2d.reshape(bsz, seq, d).astype(out_dtype)
```

```python
import functools

import jax
import jax.numpy as jnp
from jax import lax
from jax.experimental import pallas as pl
from jax.experimental.pallas import tpu as pltpu

F32 = jnp.float32
BF16 = jnp.bfloat16
I32 = jnp.int32

DEPTH = 2
DEEPNORM_ALPHA = (2.0 * DEPTH) ** 0.25
LN_EPS = 1e-5
RMS_EPS = 1e-6

A_HEADS = 4
A_DK = 128
POOL_WINDOWS = (2, 4, 8, 16)
POOL_HALO = 16
C_HEAD_DIM = 64
N_EXPERTS = 32
TOP_K = 4
SWIGLU_ALPHA = 1.702
SWIGLU_LIMIT = 7.0

HGRN_CHUNK = 16
ROW_DMA_UNROLL = 8
VMEM_LIMIT = 48 * 1024 * 1024


def _params(*sem):
    return pltpu.CompilerParams(dimension_semantics=sem, vmem_limit_bytes=VMEM_LIMIT)


def _dot(a, b):
    return jnp.dot(a, b, preferred_element_type=F32)


def _dot_nt(a, b):
    return lax.dot_general(a, b, (((1,), (1,)), ((), ())), preferred_element_type=F32)


def _dot_tn(a, b):
    return lax.dot_general(a, b, (((0,), (0,)), ((), ())), preferred_element_type=F32)


def _sigmoid(x):
    e = jnp.exp(-jnp.abs(x))
    r = 1.0 / (1.0 + e)
    return jnp.where(x >= 0, r, e * r)


def _layer_norm(y, w, b):
    mu = jnp.mean(y, axis=-1, keepdims=True)
    yc = y - mu
    var = jnp.mean(yc * yc, axis=-1, keepdims=True)
    return yc * lax.rsqrt(var + LN_EPS) * w + b


def _proj_kernel(x_ref, w_ref, o_ref):
    o_ref[...] = _dot(x_ref[...].astype(BF16), w_ref[...]).astype(o_ref.dtype)


def _proj(x2d, w, out_dtype, tm=512):
    m, k = x2d.shape
    n = w.shape[1]
    tm = min(tm, m)
    return pl.pallas_call(
        _proj_kernel,
        grid=(m // tm,),
        in_specs=[pl.BlockSpec((tm, k), lambda i: (i, 0)),
                  pl.BlockSpec((k, n), lambda i: (0, 0))],
        out_specs=pl.BlockSpec((tm, n), lambda i: (i, 0)),
        out_shape=jax.ShapeDtypeStruct((m, n), out_dtype),
        compiler_params=_params("arbitrary"),
        name="proj",
    )(x2d, w)


def _hgrn_kernel(q_ref, f_ref, i_ref, g_ref, lb_ref, nw_ref, ltri_ref, o_ref,
                 st_ref, cum_ref, kk_ref, oacc_ref, *, ts):
    c = HGRN_CHUNK

    @pl.when(pl.program_id(1) == 0)
    def _():
        st_ref[...] = jnp.zeros_like(st_ref)

    z = f_ref[...]
    lb = lb_ref[...]
    e = jnp.exp(-jnp.abs(z))
    r = 1.0 / (1.0 + e)
    er = e * r
    sig = jnp.where(z >= 0, r, er)
    sig_neg = jnp.where(z >= 0, er, r)
    logf = jnp.log(lb + (1.0 - lb) * sig)
    kk_ref[...] = (1.0 - lb) * sig_neg

    ltri = ltri_ref[...]
    x1 = logf.astype(BF16)
    r1 = logf - x1.astype(F32)
    x2 = r1.astype(BF16)
    x3 = (r1 - x2.astype(F32)).astype(BF16)
    cum_ref[...] = _dot(ltri, x1) + _dot(ltri, x2) + _dot(ltri, x3)

    t_iota = lax.broadcasted_iota(I32, (c, A_DK), 0)

    def chunk(ci, carry):
        r0 = pl.multiple_of(ci * c, c)
        for h in range(A_HEADS):
            cols = slice(h * A_DK, (h + 1) * A_DK)
            cum = cum_ref[pl.ds(r0, c), cols]
            q = q_ref[pl.ds(r0, c), cols]
            k = kk_ref[pl.ds(r0, c), cols]
            v = i_ref[pl.ds(r0, c), cols]
            last = cum[c - 1:c, :]
            st = st_ref[h]
            qd = (q * jnp.exp(cum)).astype(BF16)
            o = _dot_nt(qd, st.astype(BF16))
            for s in range(c):
                d = jnp.where(t_iota >= s, cum - cum[s:s + 1, :], -1e30)
                p = q * (k[s:s + 1, :] * jnp.exp(d))
                o = o + jnp.sum(p, axis=-1, keepdims=True) * v[s:s + 1, :]
            oacc_ref[pl.ds(r0, c), cols] = o
            kd = (k * jnp.exp(last - cum)).astype(BF16)
            st_ref[h] = jnp.exp(last) * st + _dot_tn(v.astype(BF16), kd)
        return carry

    lax.fori_loop(0, ts // c, chunk, 0)

    g = g_ref[...]
    gate = g * _sigmoid(g)
    for h in range(A_HEADS):
        cols = slice(h * A_DK, (h + 1) * A_DK)
        o = oacc_ref[:, cols]
        o = o * lax.rsqrt(jnp.mean(o * o, axis=-1, keepdims=True) + RMS_EPS) * nw_ref[...]
        o_ref[:, cols] = o * gate[:, cols]


def _hgrn(proj, lb, norm_w, bsz, seq, ts=256):
    ts = min(ts, seq)
    ns = seq // ts
    h = A_HEADS
    blk = jnp.arange(ts) // HGRN_CHUNK
    ltri = ((blk[:, None] == blk[None, :]) &
            (jnp.arange(ts)[:, None] >= jnp.arange(ts)[None, :])).astype(BF16)

    wd = h * A_DK

    def col(part):
        return pl.BlockSpec((ts, wd), lambda b, s, part=part: (b * ns + s, part))

    return pl.pallas_call(
        functools.partial(_hgrn_kernel, ts=ts),
        grid=(bsz, ns),
        in_specs=[col(0), col(1), col(2), col(3),
                  pl.BlockSpec((1, wd), lambda b, s: (0, 0)),
                  pl.BlockSpec((1, A_DK), lambda b, s: (0, 0)),
                  pl.BlockSpec((ts, ts), lambda b, s: (0, 0))],
        out_specs=pl.BlockSpec((ts, wd), lambda b, s: (b * ns + s, 0)),
        out_shape=jax.ShapeDtypeStruct((bsz * seq, wd), F32),
        scratch_shapes=[pltpu.VMEM((h, A_DK, A_DK), F32),
                        pltpu.VMEM((ts, wd), F32),
                        pltpu.VMEM((ts, wd), F32),
                        pltpu.VMEM((ts, wd), F32)],
        compiler_params=_params("arbitrary", "arbitrary"),
        name="hgrn2",
    )(proj, proj, proj, proj, lb, norm_w, ltri)


def _pool_kernel(v_ref, wp_ref, sc_ref, o_ref, buf_ref, *, ts):
    s = pl.program_id(1)
    hl = POOL_HALO

    @pl.when(s == 0)
    def _():
        buf_ref[0:hl, :] = jnp.zeros((hl, buf_ref.shape[1]), F32)

    x = v_ref[...]
    buf_ref[hl:hl + ts, :] = x
    pos = (s * ts + 1 + lax.broadcasted_iota(I32, (ts, 1), 0)).astype(F32)
    for g, win in enumerate(POOL_WINDOWS):
        lo, hi = g * 128, (g + 1) * 128
        xg = x[:, lo:hi]
        acc = xg
        for j in range(1, win):
            acc = acc + buf_ref[hl - j:hl - j + ts, lo:hi]
        pooled = acc / jnp.minimum(pos, float(win)) - xg
        o_ref[:, lo:hi] = _dot(pooled.astype(BF16), wp_ref[g]) * sc_ref[:, lo:hi]
    buf_ref[0:hl, :] = x[ts - hl:ts, :]


def _pool(proj, w_pool, pool_scale, bsz, seq, col_block, ts=512):
    ts = min(ts, seq)
    ns = seq // ts
    width = w_pool.shape[0] * w_pool.shape[1]
    return pl.pallas_call(
        functools.partial(_pool_kernel, ts=ts),
        grid=(bsz, ns),
        in_specs=[pl.BlockSpec((ts, width), lambda b, s: (b * ns + s, col_block)),
                  pl.BlockSpec(w_pool.shape, lambda b, s: (0, 0, 0)),
                  pl.BlockSpec((1, width), lambda b, s: (0, 0))],
        out_specs=pl.BlockSpec((ts, width), lambda b, s: (b * ns + s, 0)),
        out_shape=jax.ShapeDtypeStruct((bsz * seq, width), F32),
        scratch_shapes=[pltpu.VMEM((POOL_HALO + ts, width), F32)],
        compiler_params=_params("arbitrary", "arbitrary"),
        name="pool",
    )(proj, w_pool, pool_scale)


def _mix_router_kernel(*refs, n_parts, tm):
    parts = refs[:n_parts]
    wos = refs[n_parts:2 * n_parts]
    x_ref, lnw_ref, lnb_ref, wr_ref, br_ref, tri_ref = refs[2 * n_parts:2 * n_parts + 6]
    x1_ref, idx_ref, gate_ref, rank_ref, cnt_ref, run_ref = refs[2 * n_parts + 6:]

    @pl.when(pl.program_id(0) == 0)
    def _():
        run_ref[...] = jnp.zeros_like(run_ref)

    h = _dot(parts[0][...].astype(BF16), wos[0][...])
    for p_ref, w_ref in zip(parts[1:], wos[1:]):
        h = h + _dot(p_ref[...].astype(BF16), w_ref[...])
    x1 = _layer_norm(DEEPNORM_ALPHA * x_ref[...] + h, lnw_ref[...], lnb_ref[...])
    x1_ref[...] = x1

    logits = lax.dot_general(wr_ref[...], x1, (((1,), (1,)), ((), ())),
                             preferred_element_type=F32,
                             precision=lax.Precision.HIGHEST) + br_ref[...]
    ne = logits.shape[0]
    e_iota = lax.broadcasted_iota(I32, (ne, tm), 0).astype(F32)
    k_iota = lax.broadcasted_iota(I32, (TOP_K, tm), 0)
    l = logits
    vals, idxs = [], []
    for _ in range(TOP_K):
        m = jnp.max(l, axis=0, keepdims=True)
        ik = jnp.min(jnp.where(l == m, e_iota, float(ne)), axis=0, keepdims=True)
        vals.append(m)
        idxs.append(ik)
        l = jnp.where(e_iota == ik, -jnp.inf, l)
    exps = [jnp.exp(v - vals[0]) for v in vals]
    tot = exps[0] + exps[1] + exps[2] + exps[3]

    onehot = jnp.zeros((ne, tm), F32)
    for ik in idxs:
        onehot = onehot + (e_iota == ik).astype(F32)
    base = _dot(onehot.astype(BF16), tri_ref[...]) + run_ref[...]

    idx_out = jnp.zeros((TOP_K, tm), I32)
    gate_out = jnp.zeros((TOP_K, tm), F32)
    rank_out = jnp.zeros((TOP_K, tm), I32)
    for kk in range(TOP_K):
        rk = jnp.sum(jnp.where(e_iota == idxs[kk], base, 0.0), axis=0, keepdims=True)
        idx_out = jnp.where(k_iota == kk, idxs[kk].astype(I32), idx_out)
        gate_out = jnp.where(k_iota == kk, exps[kk] / tot, gate_out)
        rank_out = jnp.where(k_iota == kk, rk.astype(I32), rank_out)
    idx_ref[...] = idx_out
    gate_ref[...] = gate_out
    rank_ref[...] = rank_out
    run_ref[...] = run_ref[...] + jnp.sum(onehot, axis=1, keepdims=True)
    cnt_ref[...] = run_ref[...].astype(I32)


def _mix_router(parts, w_o, x2d, ln_w, ln_b, w_router, b_router, tm=256):
    t, d = x2d.shape
    tm = min(tm, t)
    n_parts = len(parts)
    wos, off = [], 0
    for p in parts:
        wos.append(lax.slice_in_dim(w_o, off, off + p.shape[1], axis=0))
        off += p.shape[1]
    tri = (jnp.arange(tm)[:, None] < jnp.arange(tm)[None, :]).astype(BF16)
    ne = w_router.shape[1]
    row = lambda i: (i, 0)
    col = lambda i: (0, i)
    fixed = lambda i: (0, 0)
    in_specs = ([pl.BlockSpec((tm, p.shape[1]), row) for p in parts] +
                [pl.BlockSpec(w.shape, fixed) for w in wos] +
                [pl.BlockSpec((tm, d), row),
                 pl.BlockSpec((1, d), fixed), pl.BlockSpec((1, d), fixed),
                 pl.BlockSpec((ne, d), fixed), pl.BlockSpec((ne, 1), fixed),
                 pl.BlockSpec((tm, tm), fixed)])
    out_specs = [pl.BlockSpec((tm, d), row),
                 pl.BlockSpec((TOP_K, tm), col),
                 pl.BlockSpec((TOP_K, tm), col),
                 pl.BlockSpec((TOP_K, tm), col),
                 pl.BlockSpec((ne, 1), fixed)]
    out_shape = [jax.ShapeDtypeStruct((t, d), F32),
                 jax.ShapeDtypeStruct((TOP_K, t), I32),
                 jax.ShapeDtypeStruct((TOP_K, t), F32),
                 jax.ShapeDtypeStruct((TOP_K, t), I32),
                 jax.ShapeDtypeStruct((ne, 1), I32)]
    return pl.pallas_call(
        functools.partial(_mix_router_kernel, n_parts=n_parts, tm=tm),
        grid=(t // tm,),
        in_specs=in_specs,
        out_specs=out_specs,
        out_shape=out_shape,
        scratch_shapes=[pltpu.VMEM((ne, 1), F32)],
        compiler_params=_params("arbitrary"),
        name="mix_router",
    )(*parts, *wos, x2d, ln_w, ln_b, w_router.T, b_router.reshape(ne, 1), tri)


def _dispatch_kernel(starts_ref, pc_ref, pos_ref, x_ref, xs_ref, zero_ref, sem, zsem,
                     *, tm, te):
    ne = starts_ref.shape[0]
    n_tiles = xs_ref.shape[0] // te

    def zero_tile(tile):
        return pltpu.make_async_copy(
            zero_ref, xs_ref.at[pl.ds(pl.multiple_of(tile * te, te), te)], zsem)

    def last_tile(e):
        return (starts_ref[e] + pc_ref[e]) // te - 1

    @pl.when(pl.program_id(0) == 0)
    def _():
        zero_ref[...] = jnp.zeros_like(zero_ref)
        n_used = (starts_ref[ne - 1] + pc_ref[ne - 1]) // te

        def start(e, carry):
            @pl.when(pc_ref[e] > 0)
            def _():
                zero_tile(last_tile(e)).start()
            return carry

        def wait(e, carry):
            @pl.when(pc_ref[e] > 0)
            def _():
                zero_tile(last_tile(e)).wait()
            return carry

        def start_tail(tile, carry):
            zero_tile(tile).start()
            return carry

        def wait_tail(tile, carry):
            zero_tile(tile).wait()
            return carry

        lax.fori_loop(0, ne, start, 0)
        lax.fori_loop(n_used, n_tiles, start_tail, 0)
        lax.fori_loop(0, ne, wait, 0)
        lax.fori_loop(n_used, n_tiles, wait_tail, 0)

    def row(r, carry):
        for kk in range(TOP_K):
            p = pos_ref[0, 0, kk * tm + r]
            pltpu.make_async_copy(x_ref.at[pl.ds(r, 1)], xs_ref.at[pl.ds(p, 1)], sem).start()
        return carry

    lax.fori_loop(0, tm, row, 0, unroll=ROW_DMA_UNROLL)
    for kk in range(TOP_K):
        pltpu.make_async_copy(x_ref, xs_ref.at[pl.ds(0, tm)], sem).wait()


def _dispatch(x1, pos_tiles, starts, pc, n_slots, tm, te):
    t, d = x1.shape
    grid_spec = pltpu.PrefetchScalarGridSpec(
        num_scalar_prefetch=2,
        grid=(t // tm,),
        in_specs=[pl.BlockSpec((1, 1, tm * TOP_K), lambda i, *_: (i, 0, 0),
                               memory_space=pltpu.SMEM),
                  pl.BlockSpec((tm, d), lambda i, *_: (i, 0))],
        out_specs=pl.BlockSpec(memory_space=pl.ANY),
        scratch_shapes=[pltpu.VMEM((te, d), F32),
                        pltpu.SemaphoreType.DMA,
                        pltpu.SemaphoreType.DMA],
    )
    return pl.pallas_call(
        functools.partial(_dispatch_kernel, tm=tm, te=te),
        grid_spec=grid_spec,
        out_shape=jax.ShapeDtypeStruct((n_slots, d), F32),
        compiler_params=_params("arbitrary"),
        name="moe_dispatch",
    )(starts, pc, pos_tiles, x1)


DEINT_CHUNK = 256


def _deinterleave_kernel(w_ref, p_ref, g_ref, l_ref):
    half = DEINT_CHUNK // 2
    r = _dot(w_ref[0].astype(BF16), p_ref[...])
    g_ref[0] = r[:, :half].astype(BF16)
    l_ref[0] = r[:, half:].astype(BF16)


def _deinterleave(w1):
    ne, d, two_dff = w1.shape
    c = DEINT_CHUNK
    half = c // 2
    j = jnp.arange(c)
    perm = (j[None, :] == jnp.where(j % 2 == 0, j // 2, half + j // 2)[:, None]).astype(BF16)
    out = jax.ShapeDtypeStruct((ne, d, two_dff // 2), BF16)
    return pl.pallas_call(
        _deinterleave_kernel,
        grid=(ne, two_dff // c),
        in_specs=[pl.BlockSpec((1, d, c), lambda e, k: (e, 0, k)),
                  pl.BlockSpec((c, c), lambda e, k: (0, 0))],
        out_specs=[pl.BlockSpec((1, d, half), lambda e, k: (e, 0, k)),
                   pl.BlockSpec((1, d, half), lambda e, k: (e, 0, k))],
        out_shape=[out, out],
        compiler_params=_params("arbitrary", "arbitrary"),
        name="w1_deinterleave",
    )(w1, perm)


def _expert_kernel(te_ref, nu_ref, x_ref, w1g_ref, w1l_ref, b1g_ref, b1l_ref, w2_ref, b2_ref,
                   o_ref):
    @pl.when(pl.program_id(0) < nu_ref[0])
    def _():
        xb = x_ref[...].astype(BF16)
        hg = _dot(xb, w1g_ref[0]) + b1g_ref[0]
        hl = _dot(xb, w1l_ref[0]) + b1l_ref[0]
        glu = jnp.minimum(hg, SWIGLU_LIMIT)
        lin = jnp.clip(hl, -SWIGLU_LIMIT, SWIGLU_LIMIT)
        act = glu * _sigmoid(SWIGLU_ALPHA * glu) * (lin + 1.0)
        o_ref[...] = _dot(act.astype(BF16), w2_ref[0]) + b2_ref[0]

    @pl.when(pl.program_id(0) >= nu_ref[0])
    def _():
        o_ref[...] = jnp.zeros_like(o_ref)


def _experts(xs, tile_expert, n_used, w1g, w1l, b1g, b1l, w2, b2, te):
    n_slots, d = xs.shape
    dff = w1g.shape[2]
    n_tiles = n_slots // te

    def row(i, te_ref, nu_ref):
        return (jnp.minimum(i, nu_ref[0] - 1), 0)

    def wsel(i, te_ref, nu_ref):
        return (te_ref[i], 0, 0)

    grid_spec = pltpu.PrefetchScalarGridSpec(
        num_scalar_prefetch=2,
        grid=(n_tiles,),
        in_specs=[pl.BlockSpec((te, d), row),
                  pl.BlockSpec((1, d, dff), wsel), pl.BlockSpec((1, d, dff), wsel),
                  pl.BlockSpec((1, 1, dff), wsel), pl.BlockSpec((1, 1, dff), wsel),
                  pl.BlockSpec((1, dff, d), wsel), pl.BlockSpec((1, 1, d), wsel)],
        out_specs=pl.BlockSpec((te, d), lambda i, te_ref, nu_ref: (i, 0)),
    )
    return pl.pallas_call(
        _expert_kernel,
        grid_spec=grid_spec,
        out_shape=jax.ShapeDtypeStruct((n_slots, d), F32),
        compiler_params=_params("arbitrary"),
        name="moe_experts",
    )(tile_expert, n_used, xs, w1g, w1l, b1g, b1l, w2, b2)


def _combine_kernel(pos_ref, x_ref, gate_ref, lnw_ref, lnb_ref, eo_ref, o_ref, buf_ref, sem,
                    *, tm):
    def row(r, carry):
        for kk in range(TOP_K):
            p = pos_ref[0, 0, kk * tm + r]
            pltpu.make_async_copy(eo_ref.at[pl.ds(p, 1)], buf_ref.at[kk, pl.ds(r, 1)], sem).start()
        return carry

    lax.fori_loop(0, tm, row, 0, unroll=ROW_DMA_UNROLL)
    for kk in range(TOP_K):
        pltpu.make_async_copy(eo_ref.at[pl.ds(0, tm)], buf_ref.at[kk], sem).wait()

    g = gate_ref[...]
    y = g[:, 0:1] * buf_ref[0]
    for kk in range(1, TOP_K):
        y = y + g[:, kk:kk + 1] * buf_ref[kk]
    o_ref[...] = _layer_norm(DEEPNORM_ALPHA * x_ref[...] + y, lnw_ref[...], lnb_ref[...])


def _combine(eo, pos_tiles, x1, gates, ln_w, ln_b, tm):
    t, d = x1.shape
    row = lambda i: (i, 0)
    fixed = lambda i: (0, 0)
    return pl.pallas_call(
        functools.partial(_combine_kernel, tm=tm),
        grid=(t // tm,),
        in_specs=[pl.BlockSpec((1, 1, tm * TOP_K), lambda i: (i, 0, 0), memory_space=pltpu.SMEM),
                  pl.BlockSpec((tm, d), row),
                  pl.BlockSpec((tm, TOP_K), row),
                  pl.BlockSpec((1, d), fixed), pl.BlockSpec((1, d), fixed),
                  pl.BlockSpec(memory_space=pl.ANY)],
        out_specs=pl.BlockSpec((tm, d), row),
        out_shape=jax.ShapeDtypeStruct((t, d), F32),
        scratch_shapes=[pltpu.VMEM((TOP_K, tm, d), F32), pltpu.SemaphoreType.DMA],
        compiler_params=_params("arbitrary"),
        name="moe_combine",
    )(pos_tiles, x1, gates, ln_w, ln_b, eo)


def _moe(x1, idx, gates, rank, counts, w1, b1, w2, b2, ln_w, ln_b, tm=256, te=256):
    t, d = x1.shape
    tm = min(tm, t)
    ne = w1.shape[0]
    counts = counts[:, 0]
    pc = ((counts + te - 1) // te) * te
    ends = jnp.cumsum(pc)
    starts = ends - pc
    n_tiles = (t * TOP_K) // te + ne
    n_used = (ends[-1] // te).astype(I32)
    tile_ids = jnp.arange(n_tiles, dtype=I32)
    te_raw = jnp.sum((tile_ids[:, None] >= (ends // te)[None, :]).astype(I32), axis=1)
    te_raw = jnp.minimum(te_raw, ne - 1)
    tile_expert = jnp.where(tile_ids < n_used, te_raw, jnp.take(te_raw, n_used - 1))
    pos = jnp.take(starts, idx) + rank
    pos_tiles = pos.reshape(TOP_K, t // tm, tm).transpose(1, 0, 2).reshape(t // tm, 1, TOP_K * tm)
    gates = gates.T

    w1g, w1l = _deinterleave(w1)
    b1g = b1[:, None, 0::2]
    b1l = b1[:, None, 1::2]

    xs = _dispatch(x1, pos_tiles, starts.astype(I32), pc.astype(I32), n_tiles * te, tm, te)
    eo = _experts(xs, tile_expert, n_used.reshape(1), w1g, w1l, b1g, b1l,
                  w2.astype(BF16), b2[:, None, :], te)
    return _combine(eo, pos_tiles, x1, gates, ln_w, ln_b, tm)


ATTN_GROUP = 4


def _attn_kernel(q_ref, k_ref, v_ref, nu_ref, o_ref, qs_ref, z0_ref, z1_ref, sp0_ref, sp1_ref,
                 w0_ref, w1_ref, acc_ref, carry_ref, *, tq):
    z_refs, sp_refs, w_refs = (z0_ref, z1_ref), (sp0_ref, sp1_ref), (w0_ref, w1_ref)
    i = pl.program_id(2)
    hd = C_HEAD_DIM
    g = ATTN_GROUP
    width = g * hd
    q = q_ref[...]
    lane = lax.broadcasted_iota(I32, (tq, width), 1)
    for h in range(g):
        in_head = (lane >= h * hd) & (lane < (h + 1) * hd)
        qs_ref[h * tq:(h + 1) * tq, :] = jnp.where(in_head, q, jnp.zeros_like(q))
    neg_u = nu_ref[...]
    row = lax.broadcasted_iota(I32, (g * tq, tq), 0) & (tq - 1)
    strict = lax.broadcasted_iota(I32, (g * tq, tq), 1) < row
    vmask = [(lane >= h * hd) & (lane < (h + 1) * hd) for h in range(g)]

    def key_rows(n):
        return pl.ds(pl.multiple_of(jnp.maximum(i - n, 0) * tq, tq), tq)

    def scores(n, slot, diag):
        z = _dot_nt(qs_ref[...], k_ref[key_rows(n), :])
        z_refs[slot][...] = z
        zb = z.astype(BF16)
        sp = jnp.maximum(zb, 0) + jnp.log(1 + jnp.exp(-jnp.abs(zb)))
        if diag:
            sp = jnp.where(strict, sp, jnp.zeros_like(sp))
        sp_refs[slot][...] = sp

    def weights(slot, diag):
        suf = _dot(sp_refs[slot][...], neg_u)
        w = jnp.exp(z_refs[slot][...] + suf + carry_ref[...])
        if diag:
            w = jnp.where(strict, w, 0.0)
        w = w.astype(BF16)
        for h in range(g):
            w_refs[slot][:, h * tq:(h + 1) * tq] = w[h * tq:(h + 1) * tq, :]
        carry_ref[...] = carry_ref[...] + suf[:, 0:1]

    def output(n, slot, valid=None):
        vj = v_ref[key_rows(n), :]
        v_cat = jnp.concatenate([jnp.where(m, vj, jnp.zeros_like(vj)) for m in vmask], axis=0)
        pv = _dot(w_refs[slot][...], v_cat)
        if valid is not None:
            pv = jnp.where(valid, pv, 0.0)
        acc_ref[...] = acc_ref[...] + pv

    carry_ref[...] = jnp.zeros_like(carry_ref)
    acc_ref[...] = jnp.zeros_like(acc_ref)
    scores(0, 0, True)
    weights(0, True)
    scores(1, 1, False)

    def body(p, carry):
        t = 2 + 2 * p
        output(t - 2, 0)
        weights(1, False)
        scores(t, 0, False)
        output(t - 1, 1, valid=(t - 1 <= i))
        weights(0, False)
        scores(t + 1, 1, False)
        return carry

    lax.fori_loop(0, (i + 2) // 2, body, 0)
    o_ref[...] = acc_ref[...].astype(o_ref.dtype)


def _attention(qkv, bsz, seq, width, tq=256):
    tq = min(tq, seq)
    blk = ATTN_GROUP * C_HEAD_DIM
    ngroups = width // blk
    nq = seq // tq
    neg_u = -(jnp.arange(tq)[:, None] >= jnp.arange(tq)[None, :]).astype(BF16)
    qkv3 = qkv.reshape(bsz, seq, 3 * width)
    return pl.pallas_call(
        functools.partial(_attn_kernel, tq=tq),
        grid=(bsz, ngroups, nq),
        in_specs=[pl.BlockSpec((None, tq, blk), lambda b, p, i: (b, i, p)),
                  pl.BlockSpec((None, seq, blk), lambda b, p, i: (b, 0, ngroups + p)),
                  pl.BlockSpec((None, seq, blk), lambda b, p, i: (b, 0, 2 * ngroups + p)),
                  pl.BlockSpec((tq, tq), lambda b, p, i: (0, 0))],
        out_specs=pl.BlockSpec((None, tq, blk), lambda b, p, i: (b, i, p)),
        out_shape=jax.ShapeDtypeStruct((bsz, seq, width), BF16),
        scratch_shapes=[pltpu.VMEM((ATTN_GROUP * tq, blk), BF16),
                        pltpu.VMEM((ATTN_GROUP * tq, tq), F32),
                        pltpu.VMEM((ATTN_GROUP * tq, tq), F32),
                        pltpu.VMEM((ATTN_GROUP * tq, tq), BF16),
                        pltpu.VMEM((ATTN_GROUP * tq, tq), BF16),
                        pltpu.VMEM((tq, ATTN_GROUP * tq), BF16),
                        pltpu.VMEM((tq, ATTN_GROUP * tq), BF16),
                        pltpu.VMEM((tq, blk), F32),
                        pltpu.VMEM((ATTN_GROUP * tq, 1), F32)],
        compiler_params=_params("arbitrary", "arbitrary", "arbitrary"),
        name="stickbreak_attn",
    )(qkv3, qkv3, qkv3, neg_u).reshape(bsz * seq, width)


def kernel(x, hgrn_lb_logits, l0_w_in, l0_a_norm_w, l0_w_pool, l0_pool_scale, l0_w_o, l0_ln1_w, l0_ln1_b, l0_w_router, l0_b_router, l0_w1, l0_b1, l0_w2, l0_b2, l0_ln2_w, l0_ln2_b, l1_w_in, l1_w_o, l1_ln1_w, l1_ln1_b, l1_w_router, l1_b_router, l1_w1, l1_b1, l1_w2, l1_b2, l1_ln2_w, l1_ln2_b):
    bsz, seq, d = x.shape
    out_dtype = x.dtype
    x2d = x.reshape(bsz * seq, d).astype(F32)
    r1 = lambda a: a.reshape(1, -1).astype(F32)

    a_width = A_HEADS * A_DK
    lb = jnp.cumsum(jax.nn.softmax(hgrn_lb_logits.astype(F32), axis=0), axis=0)[0]
    proj0 = _proj(x2d, l0_w_in.astype(BF16), F32)
    a_out = _hgrn(proj0, r1(lb), r1(l0_a_norm_w), bsz, seq)
    b_width = l0_w_pool.shape[0] * l0_w_pool.shape[1]
    b_out = _pool(proj0, l0_w_pool.astype(BF16), r1(l0_pool_scale), bsz, seq,
                  col_block=(4 * a_width) // b_width)
    x1, idx, gates, rank, counts = _mix_router(
        [a_out, b_out], l0_w_o.astype(BF16), x2d, r1(l0_ln1_w), r1(l0_ln1_b),
        l0_w_router.astype(F32), r1(l0_b_router))
    x2d = _moe(x1, idx, gates, rank, counts, l0_w1, l0_b1, l0_w2, l0_b2,
               r1(l0_ln2_w), r1(l0_ln2_b))

    c_width = l1_w_in.shape[1] // 3
    qscale = jnp.concatenate([jnp.full((c_width,), C_HEAD_DIM ** -0.5, F32),
                              jnp.ones((2 * c_width,), F32)])
    qkv = _proj(x2d, (l1_w_in * qscale).astype(BF16), BF16)
    o = _attention(qkv, bsz, seq, c_width)
    x1, idx, gates, rank, counts = _mix_router(
        [o], l1_w_o.astype(BF16), x2d, r1(l1_ln1_w), r1(l1_ln1_b),
        l1_w_router.astype(F32), r1(l1_b_router))
    x2d = _moe(x1, idx, gates, rank, counts, l1_w1, l1_b1, l1_w2, l1_b2,
               r1(l1_ln2_w), r1(l1_ln2_b))
    return x2d.reshape(bsz, seq, d).astype(out_dtype)
```

```python
import functools

import jax
import jax.numpy as jnp
from jax import lax
from jax.experimental import pallas as pl
from jax.experimental.pallas import tpu as pltpu

F32 = jnp.float32
BF16 = jnp.bfloat16
I32 = jnp.int32

DEPTH = 2
DEEPNORM_ALPHA = (2.0 * DEPTH) ** 0.25
LN_EPS = 1e-5
RMS_EPS = 1e-6

A_HEADS = 4
A_DK = 128
POOL_WINDOWS = (2, 4, 8, 16)
POOL_HALO = 16
C_HEAD_DIM = 64
N_EXPERTS = 32
TOP_K = 4
SWIGLU_ALPHA = 1.702
SWIGLU_LIMIT = 7.0

HGRN_CHUNK = 16
ROW_DMA_UNROLL = 8
VMEM_LIMIT = 48 * 1024 * 1024


def _params(*sem):
    return pltpu.CompilerParams(dimension_semantics=sem, vmem_limit_bytes=VMEM_LIMIT)


def _dot(a, b):
    return jnp.dot(a, b, preferred_element_type=F32)


def _dot_nt(a, b):
    return lax.dot_general(a, b, (((1,), (1,)), ((), ())), preferred_element_type=F32)


def _dot_tn(a, b):
    return lax.dot_general(a, b, (((0,), (0,)), ((), ())), preferred_element_type=F32)


def _sigmoid(x):
    e = jnp.exp(-jnp.abs(x))
    r = 1.0 / (1.0 + e)
    return jnp.where(x >= 0, r, e * r)


def _layer_norm(y, w, b):
    mu = jnp.mean(y, axis=-1, keepdims=True)
    yc = y - mu
    var = jnp.mean(yc * yc, axis=-1, keepdims=True)
    return yc * lax.rsqrt(var + LN_EPS) * w + b


def _proj_kernel(x_ref, w_ref, o_ref):
    o_ref[...] = _dot(x_ref[...].astype(BF16), w_ref[...]).astype(o_ref.dtype)


def _proj(x2d, w, out_dtype, tm=512):
    m, k = x2d.shape
    n = w.shape[1]
    tm = min(tm, m)
    return pl.pallas_call(
        _proj_kernel,
        grid=(m // tm,),
        in_specs=[pl.BlockSpec((tm, k), lambda i: (i, 0)),
                  pl.BlockSpec((k, n), lambda i: (0, 0))],
        out_specs=pl.BlockSpec((tm, n), lambda i: (i, 0)),
        out_shape=jax.ShapeDtypeStruct((m, n), out_dtype),
        compiler_params=_params("arbitrary"),
        name="proj",
    )(x2d, w)


def _hgrn_kernel(q_ref, f_ref, i_ref, g_ref, lb_ref, nw_ref, ltri_ref, o_ref,
                 st_ref, cum_ref, kk_ref, oacc_ref, *, ts):
    c = HGRN_CHUNK

    @pl.when(pl.program_id(1) == 0)
    def _():
        st_ref[...] = jnp.zeros_like(st_ref)

    z = f_ref[...]
    lb = lb_ref[...]
    e = jnp.exp(-jnp.abs(z))
    r = 1.0 / (1.0 + e)
    er = e * r
    sig = jnp.where(z >= 0, r, er)
    sig_neg = jnp.where(z >= 0, er, r)
    logf = jnp.log(lb + (1.0 - lb) * sig)
    kk_ref[...] = (1.0 - lb) * sig_neg

    ltri = ltri_ref[...]
    x1 = logf.astype(BF16)
    r1 = logf - x1.astype(F32)
    x2 = r1.astype(BF16)
    x3 = (r1 - x2.astype(F32)).astype(BF16)
    cum_ref[...] = _dot(ltri, x1) + _dot(ltri, x2) + _dot(ltri, x3)

    t_iota = lax.broadcasted_iota(I32, (c, A_DK), 0)

    def chunk(ci, carry):
        r0 = pl.multiple_of(ci * c, c)
        for h in range(A_HEADS):
            cols = slice(h * A_DK, (h + 1) * A_DK)
            cum = cum_ref[pl.ds(r0, c), cols]
            q = q_ref[pl.ds(r0, c), cols]
            k = kk_ref[pl.ds(r0, c), cols]
            v = i_ref[pl.ds(r0, c), cols]
            last = cum[c - 1:c, :]
            st = st_ref[h]
            qd = (q * jnp.exp(cum)).astype(BF16)
            o = _dot_nt(qd, st.astype(BF16))
            for s in range(c):
                d = jnp.where(t_iota >= s, cum - cum[s:s + 1, :], -1e30)
                p = q * (k[s:s + 1, :] * jnp.exp(d))
                o = o + jnp.sum(p, axis=-1, keepdims=True) * v[s:s + 1, :]
            oacc_ref[pl.ds(r0, c), cols] = o
            kd = (k * jnp.exp(last - cum)).astype(BF16)
            st_ref[h] = jnp.exp(last) * st + _dot_tn(v.astype(BF16), kd)
        return carry

    lax.fori_loop(0, ts // c, chunk, 0)

    g = g_ref[...]
    gate = g * _sigmoid(g)
    for h in range(A_HEADS):
        cols = slice(h * A_DK, (h + 1) * A_DK)
        o = oacc_ref[:, cols]
        o = o * lax.rsqrt(jnp.mean(o * o, axis=-1, keepdims=True) + RMS_EPS) * nw_ref[...]
        o_ref[:, cols] = o * gate[:, cols]


def _hgrn(proj, lb, norm_w, bsz, seq, ts=256):
    ts = min(ts, seq)
    ns = seq // ts
    h = A_HEADS
    blk = jnp.arange(ts) // HGRN_CHUNK
    ltri = ((blk[:, None] == blk[None, :]) &
            (jnp.arange(ts)[:, None] >= jnp.arange(ts)[None, :])).astype(BF16)

    wd = h * A_DK

    def col(part):
        return pl.BlockSpec((ts, wd), lambda b, s, part=part: (b * ns + s, part))

    return pl.pallas_call(
        functools.partial(_hgrn_kernel, ts=ts),
        grid=(bsz, ns),
        in_specs=[col(0), col(1), col(2), col(3),
                  pl.BlockSpec((1, wd), lambda b, s: (0, 0)),
                  pl.BlockSpec((1, A_DK), lambda b, s: (0, 0)),
                  pl.BlockSpec((ts, ts), lambda b, s: (0, 0))],
        out_specs=pl.BlockSpec((ts, wd), lambda b, s: (b * ns + s, 0)),
        out_shape=jax.ShapeDtypeStruct((bsz * seq, wd), F32),
        scratch_shapes=[pltpu.VMEM((h, A_DK, A_DK), F32),
                        pltpu.VMEM((ts, wd), F32),
                        pltpu.VMEM((ts, wd), F32),
                        pltpu.VMEM((ts, wd), F32)],
        compiler_params=_params("arbitrary", "arbitrary"),
        name="hgrn2",
    )(proj, proj, proj, proj, lb, norm_w, ltri)


def _pool_kernel(v_ref, wp_ref, sc_ref, o_ref, buf_ref, *, ts):
    s = pl.program_id(1)
    hl = POOL_HALO

    @pl.when(s == 0)
    def _():
        buf_ref[0:hl, :] = jnp.zeros((hl, buf_ref.shape[1]), F32)

    x = v_ref[...]
    buf_ref[hl:hl + ts, :] = x
    pos = (s * ts + 1 + lax.broadcasted_iota(I32, (ts, 1), 0)).astype(F32)
    for g, win in enumerate(POOL_WINDOWS):
        lo, hi = g * 128, (g + 1) * 128
        xg = x[:, lo:hi]
        acc = xg
        for j in range(1, win):
            acc = acc + buf_ref[hl - j:hl - j + ts, lo:hi]
        pooled = acc / jnp.minimum(pos, float(win)) - xg
        o_ref[:, lo:hi] = _dot(pooled.astype(BF16), wp_ref[g]) * sc_ref[:, lo:hi]
    buf_ref[0:hl, :] = x[ts - hl:ts, :]


def _pool(proj, w_pool, pool_scale, bsz, seq, col_block, ts=512):
    ts = min(ts, seq)
    ns = seq // ts
    width = w_pool.shape[0] * w_pool.shape[1]
    return pl.pallas_call(
        functools.partial(_pool_kernel, ts=ts),
        grid=(bsz, ns),
        in_specs=[pl.BlockSpec((ts, width), lambda b, s: (b * ns + s, col_block)),
                  pl.BlockSpec(w_pool.shape, lambda b, s: (0, 0, 0)),
                  pl.BlockSpec((1, width), lambda b, s: (0, 0))],
        out_specs=pl.BlockSpec((ts, width), lambda b, s: (b * ns + s, 0)),
        out_shape=jax.ShapeDtypeStruct((bsz * seq, width), F32),
        scratch_shapes=[pltpu.VMEM((POOL_HALO + ts, width), F32)],
        compiler_params=_params("arbitrary", "arbitrary"),
        name="pool",
    )(proj, w_pool, pool_scale)


def _mix_router_kernel(*refs, n_parts, tm):
    parts = refs[:n_parts]
    wos = refs[n_parts:2 * n_parts]
    x_ref, lnw_ref, lnb_ref, wr_ref, br_ref, tri_ref = refs[2 * n_parts:2 * n_parts + 6]
    x1_ref, idx_ref, gate_ref, rank_ref, cnt_ref, run_ref = refs[2 * n_parts + 6:]

    @pl.when(pl.program_id(0) == 0)
    def _():
        run_ref[...] = jnp.zeros_like(run_ref)

    h = _dot(parts[0][...].astype(BF16), wos[0][...])
    for p_ref, w_ref in zip(parts[1:], wos[1:]):
        h = h + _dot(p_ref[...].astype(BF16), w_ref[...])
    x1 = _layer_norm(DEEPNORM_ALPHA * x_ref[...] + h, lnw_ref[...], lnb_ref[...])
    x1_ref[...] = x1

    logits = lax.dot_general(wr_ref[...], x1, (((1,), (1,)), ((), ())),
                             preferred_element_type=F32,
                             precision=lax.Precision.HIGHEST) + br_ref[...]
    ne = logits.shape[0]
    e_iota = lax.broadcasted_iota(I32, (ne, tm), 0).astype(F32)
    k_iota = lax.broadcasted_iota(I32, (TOP_K, tm), 0)
    l = logits
    vals, idxs = [], []
    for _ in range(TOP_K):
        m = jnp.max(l, axis=0, keepdims=True)
        ik = jnp.min(jnp.where(l == m, e_iota, float(ne)), axis=0, keepdims=True)
        vals.append(m)
        idxs.append(ik)
        l = jnp.where(e_iota == ik, -jnp.inf, l)
    exps = [jnp.exp(v - vals[0]) for v in vals]
    tot = exps[0] + exps[1] + exps[2] + exps[3]

    onehot = jnp.zeros((ne, tm), F32)
    for ik in idxs:
        onehot = onehot + (e_iota == ik).astype(F32)
    base = _dot(onehot.astype(BF16), tri_ref[...]) + run_ref[...]

    idx_out = jnp.zeros((TOP_K, tm), I32)
    gate_out = jnp.zeros((TOP_K, tm), F32)
    rank_out = jnp.zeros((TOP_K, tm), I32)
    for kk in range(TOP_K):
        rk = jnp.sum(jnp.where(e_iota == idxs[kk], base, 0.0), axis=0, keepdims=True)
        idx_out = jnp.where(k_iota == kk, idxs[kk].astype(I32), idx_out)
        gate_out = jnp.where(k_iota == kk, exps[kk] / tot, gate_out)
        rank_out = jnp.where(k_iota == kk, rk.astype(I32), rank_out)
    idx_ref[...] = idx_out
    gate_ref[...] = gate_out
    rank_ref[...] = rank_out
    run_ref[...] = run_ref[...] + jnp.sum(onehot, axis=1, keepdims=True)
    cnt_ref[...] = run_ref[...].astype(I32)


def _mix_router(parts, w_o, x2d, ln_w, ln_b, w_router, b_router, tm=256):
    t, d = x2d.shape
    tm = min(tm, t)
    n_parts = len(parts)
    wos, off = [], 0
    for p in parts:
        wos.append(lax.slice_in_dim(w_o, off, off + p.shape[1], axis=0))
        off += p.shape[1]
    tri = (jnp.arange(tm)[:, None] < jnp.arange(tm)[None, :]).astype(BF16)
    ne = w_router.shape[1]
    row = lambda i: (i, 0)
    col = lambda i: (0, i)
    fixed = lambda i: (0, 0)
    in_specs = ([pl.BlockSpec((tm, p.shape[1]), row) for p in parts] +
                [pl.BlockSpec(w.shape, fixed) for w in wos] +
                [pl.BlockSpec((tm, d), row),
                 pl.BlockSpec((1, d), fixed), pl.BlockSpec((1, d), fixed),
                 pl.BlockSpec((ne, d), fixed), pl.BlockSpec((ne, 1), fixed),
                 pl.BlockSpec((tm, tm), fixed)])
    out_specs = [pl.BlockSpec((tm, d), row),
                 pl.BlockSpec((TOP_K, tm), col),
                 pl.BlockSpec((TOP_K, tm), col),
                 pl.BlockSpec((TOP_K, tm), col),
                 pl.BlockSpec((ne, 1), fixed)]
    out_shape = [jax.ShapeDtypeStruct((t, d), F32),
                 jax.ShapeDtypeStruct((TOP_K, t), I32),
                 jax.ShapeDtypeStruct((TOP_K, t), F32),
                 jax.ShapeDtypeStruct((TOP_K, t), I32),
                 jax.ShapeDtypeStruct((ne, 1), I32)]
    return pl.pallas_call(
        functools.partial(_mix_router_kernel, n_parts=n_parts, tm=tm),
        grid=(t // tm,),
        in_specs=in_specs,
        out_specs=out_specs,
        out_shape=out_shape,
        scratch_shapes=[pltpu.VMEM((ne, 1), F32)],
        compiler_params=_params("arbitrary"),
        name="mix_router",
    )(*parts, *wos, x2d, ln_w, ln_b, w_router.T, b_router.reshape(ne, 1), tri)


ROW_SUB = 8
ROW_LANE = 128


def _store_rows_as_tiles(ref, x, n):
    for s in range(ROW_SUB):
        ref[pl.ds(s, n, stride=ROW_SUB), :] = x[:, s * ROW_LANE:(s + 1) * ROW_LANE]


def _load_tiles_as_rows(ref, n):
    return jnp.concatenate([ref[pl.ds(s, n, stride=ROW_SUB), :] for s in range(ROW_SUB)], axis=1)


def _tile_of_row(ref, r):
    return ref.at[pl.ds(pl.multiple_of(r * ROW_SUB, ROW_SUB), ROW_SUB)]


def _dispatch_kernel(starts_ref, pc_ref, pos_ref, x_ref, xs_ref, zero_ref, stage_ref, sem, zsem,
                     *, tm, te):
    ne = starts_ref.shape[0]
    n_tiles = xs_ref.shape[0] // (te * ROW_SUB)
    step = pl.program_id(0)
    slot = step % 2

    def zero_tile(tile):
        rows = te * ROW_SUB
        return pltpu.make_async_copy(
            zero_ref, xs_ref.at[pl.ds(pl.multiple_of(tile * rows, rows), rows)], zsem)

    def wait_rows():
        for kk in range(TOP_K):
            pltpu.make_async_copy(stage_ref.at[0], xs_ref.at[pl.ds(0, tm * ROW_SUB)], sem).wait()

    def last_tile(e):
        return (starts_ref[e] + pc_ref[e]) // te - 1

    @pl.when(pl.program_id(0) == 0)
    def _():
        zero_ref[...] = jnp.zeros_like(zero_ref)
        n_used = (starts_ref[ne - 1] + pc_ref[ne - 1]) // te

        def start(e, carry):
            @pl.when(pc_ref[e] > 0)
            def _():
                zero_tile(last_tile(e)).start()
            return carry

        def wait(e, carry):
            @pl.when(pc_ref[e] > 0)
            def _():
                zero_tile(last_tile(e)).wait()
            return carry

        def start_tail(tile, carry):
            zero_tile(tile).start()
            return carry

        def wait_tail(tile, carry):
            zero_tile(tile).wait()
            return carry

        lax.fori_loop(0, ne, start, 0)
        lax.fori_loop(n_used, n_tiles, start_tail, 0)
        lax.fori_loop(0, ne, wait, 0)
        lax.fori_loop(n_used, n_tiles, wait_tail, 0)

    stage = stage_ref.at[slot]
    _store_rows_as_tiles(stage, x_ref[...], tm)

    @pl.when(step > 0)
    def _():
        wait_rows()

    def row(r, carry):
        for kk in range(TOP_K):
            p = pos_ref[0, 0, kk * tm + r]
            pltpu.make_async_copy(_tile_of_row(stage, r), _tile_of_row(xs_ref, p), sem).start()
        return carry

    lax.fori_loop(0, tm, row, 0, unroll=ROW_DMA_UNROLL)

    @pl.when(step == pl.num_programs(0) - 1)
    def _():
        wait_rows()


def _dispatch(x1, pos_tiles, starts, pc, n_slots, tm, te):
    t, d = x1.shape
    assert d == ROW_SUB * ROW_LANE
    grid_spec = pltpu.PrefetchScalarGridSpec(
        num_scalar_prefetch=2,
        grid=(t // tm,),
        in_specs=[pl.BlockSpec((1, 1, tm * TOP_K), lambda i, *_: (i, 0, 0),
                               memory_space=pltpu.SMEM),
                  pl.BlockSpec((tm, d), lambda i, *_: (i, 0))],
        out_specs=pl.BlockSpec(memory_space=pl.ANY),
        scratch_shapes=[pltpu.VMEM((te * ROW_SUB, ROW_LANE), F32),
                        pltpu.VMEM((2, tm * ROW_SUB, ROW_LANE), F32),
                        pltpu.SemaphoreType.DMA,
                        pltpu.SemaphoreType.DMA],
    )
    return pl.pallas_call(
        functools.partial(_dispatch_kernel, tm=tm, te=te),
        grid_spec=grid_spec,
        out_shape=jax.ShapeDtypeStruct((n_slots * ROW_SUB, ROW_LANE), F32),
        compiler_params=_params("arbitrary"),
        name="moe_dispatch",
    )(starts, pc, pos_tiles, x1)


DEINT_CHUNK = 256


def _deinterleave_kernel(w_ref, p_ref, g_ref, l_ref):
    half = DEINT_CHUNK // 2
    r = _dot(w_ref[0].astype(BF16), p_ref[...])
    g_ref[0] = r[:, :half].astype(BF16)
    l_ref[0] = r[:, half:].astype(BF16)


def _deinterleave(w1):
    ne, d, two_dff = w1.shape
    c = DEINT_CHUNK
    half = c // 2
    j = jnp.arange(c)
    perm = (j[None, :] == jnp.where(j % 2 == 0, j // 2, half + j // 2)[:, None]).astype(BF16)
    out = jax.ShapeDtypeStruct((ne, d, two_dff // 2), BF16)
    return pl.pallas_call(
        _deinterleave_kernel,
        grid=(ne, two_dff // c),
        in_specs=[pl.BlockSpec((1, d, c), lambda e, k: (e, 0, k)),
                  pl.BlockSpec((c, c), lambda e, k: (0, 0))],
        out_specs=[pl.BlockSpec((1, d, half), lambda e, k: (e, 0, k)),
                   pl.BlockSpec((1, d, half), lambda e, k: (e, 0, k))],
        out_shape=[out, out],
        compiler_params=_params("arbitrary", "arbitrary"),
        name="w1_deinterleave",
    )(w1, perm)


def _expert_kernel(te_ref, nu_ref, x_ref, w1g_ref, w1l_ref, b1g_ref, b1l_ref, w2_ref, b2_ref,
                   o_ref, *, te):
    @pl.when(pl.program_id(0) < nu_ref[0])
    def _():
        xb = _load_tiles_as_rows(x_ref, te).astype(BF16)
        hg = _dot(xb, w1g_ref[0]) + b1g_ref[0]
        hl = _dot(xb, w1l_ref[0]) + b1l_ref[0]
        glu = jnp.minimum(hg, SWIGLU_LIMIT)
        lin = jnp.clip(hl, -SWIGLU_LIMIT, SWIGLU_LIMIT)
        act = glu * _sigmoid(SWIGLU_ALPHA * glu) * (lin + 1.0)
        _store_rows_as_tiles(o_ref, _dot(act.astype(BF16), w2_ref[0]) + b2_ref[0], te)

    @pl.when(pl.program_id(0) >= nu_ref[0])
    def _():
        o_ref[...] = jnp.zeros_like(o_ref)


def _experts(xs, tile_expert, n_used, w1g, w1l, b1g, b1l, w2, b2, te):
    d = w1g.shape[1]
    dff = w1g.shape[2]
    n_tiles = xs.shape[0] // (te * ROW_SUB)
    blk = (te * ROW_SUB, ROW_LANE)

    def row(i, te_ref, nu_ref):
        return (jnp.minimum(i, nu_ref[0] - 1), 0)

    def wsel(i, te_ref, nu_ref):
        return (te_ref[i], 0, 0)

    grid_spec = pltpu.PrefetchScalarGridSpec(
        num_scalar_prefetch=2,
        grid=(n_tiles,),
        in_specs=[pl.BlockSpec(blk, row),
                  pl.BlockSpec((1, d, dff), wsel), pl.BlockSpec((1, d, dff), wsel),
                  pl.BlockSpec((1, 1, dff), wsel), pl.BlockSpec((1, 1, dff), wsel),
                  pl.BlockSpec((1, dff, d), wsel), pl.BlockSpec((1, 1, d), wsel)],
        out_specs=pl.BlockSpec(blk, lambda i, te_ref, nu_ref: (i, 0)),
    )
    return pl.pallas_call(
        functools.partial(_expert_kernel, te=te),
        grid_spec=grid_spec,
        out_shape=jax.ShapeDtypeStruct(xs.shape, F32),
        compiler_params=_params("arbitrary"),
        name="moe_experts",
    )(tile_expert, n_used, xs, w1g, w1l, b1g, b1l, w2, b2)


def _combine_kernel(pos_ref, pos_next_ref, x_ref, gate_ref, lnw_ref, lnb_ref, eo_ref, o_ref,
                    buf_ref, sem, *, tm):
    step = pl.program_id(0)
    slot = step % 2

    def gather(p_ref, slot_):
        def row(r, carry):
            for kk in range(TOP_K):
                p = p_ref[0, 0, kk * tm + r]
                pltpu.make_async_copy(_tile_of_row(eo_ref, p),
                                      _tile_of_row(buf_ref.at[slot_, kk], r),
                                      sem.at[slot_]).start()
            return carry

        lax.fori_loop(0, tm, row, 0, unroll=ROW_DMA_UNROLL)

    @pl.when(step == 0)
    def _():
        gather(pos_ref, slot)

    @pl.when(step + 1 < pl.num_programs(0))
    def _():
        gather(pos_next_ref, 1 - slot)

    for kk in range(TOP_K):
        pltpu.make_async_copy(eo_ref.at[pl.ds(0, tm * ROW_SUB)], buf_ref.at[slot, kk],
                              sem.at[slot]).wait()

    g = gate_ref[...]
    y = g[:, 0:1] * _load_tiles_as_rows(buf_ref.at[slot, 0], tm)
    for kk in range(1, TOP_K):
        y = y + g[:, kk:kk + 1] * _load_tiles_as_rows(buf_ref.at[slot, kk], tm)
    o_ref[...] = _layer_norm(DEEPNORM_ALPHA * x_ref[...] + y, lnw_ref[...], lnb_ref[...])


def _combine(eo, pos_tiles, x1, gates, ln_w, ln_b, tm):
    t, d = x1.shape
    nt = t // tm
    row = lambda i: (i, 0)
    fixed = lambda i: (0, 0)
    pos_spec = lambda f: pl.BlockSpec((1, 1, tm * TOP_K), f, memory_space=pltpu.SMEM)
    return pl.pallas_call(
        functools.partial(_combine_kernel, tm=tm),
        grid=(nt,),
        in_specs=[pos_spec(lambda i: (i, 0, 0)),
                  pos_spec(lambda i: (jnp.minimum(i + 1, nt - 1), 0, 0)),
                  pl.BlockSpec((tm, d), row),
                  pl.BlockSpec((tm, TOP_K), row),
                  pl.BlockSpec((1, d), fixed), pl.BlockSpec((1, d), fixed),
                  pl.BlockSpec(memory_space=pl.ANY)],
        out_specs=pl.BlockSpec((tm, d), row),
        out_shape=jax.ShapeDtypeStruct((t, d), F32),
        scratch_shapes=[pltpu.VMEM((2, TOP_K, tm * ROW_SUB, ROW_LANE), F32),
                        pltpu.SemaphoreType.DMA((2,))],
        compiler_params=_params("arbitrary"),
        name="moe_combine",
    )(pos_tiles, pos_tiles, x1, gates, ln_w, ln_b, eo)


def _moe(x1, idx, gates, rank, counts, w1, b1, w2, b2, ln_w, ln_b, tm=256, te=256):
    t, d = x1.shape
    tm = min(tm, t)
    ne = w1.shape[0]
    counts = counts[:, 0]
    pc = ((counts + te - 1) // te) * te
    ends = jnp.cumsum(pc)
    starts = ends - pc
    n_tiles = (t * TOP_K) // te + ne
    n_used = (ends[-1] // te).astype(I32)
    tile_ids = jnp.arange(n_tiles, dtype=I32)
    te_raw = jnp.sum((tile_ids[:, None] >= (ends // te)[None, :]).astype(I32), axis=1)
    te_raw = jnp.minimum(te_raw, ne - 1)
    tile_expert = jnp.where(tile_ids < n_used, te_raw, jnp.take(te_raw, n_used - 1))
    e_ids = jnp.arange(ne, dtype=I32)[:, None, None]
    start_of = jnp.sum(jnp.where(idx[None] == e_ids, starts.astype(I32)[:, None, None], 0), axis=0)
    pos = start_of + rank
    pos_tiles = pos.reshape(TOP_K, t // tm, tm).transpose(1, 0, 2).reshape(t // tm, 1, TOP_K * tm)
    gates = gates.T

    w1g, w1l = _deinterleave(w1)
    b1g = b1[:, None, 0::2]
    b1l = b1[:, None, 1::2]

    xs = _dispatch(x1, pos_tiles, starts.astype(I32), pc.astype(I32), n_tiles * te, tm, te)
    eo = _experts(xs, tile_expert, n_used.reshape(1), w1g, w1l, b1g, b1l,
                  w2.astype(BF16), b2[:, None, :], te)
    return _combine(eo, pos_tiles, x1, gates, ln_w, ln_b, tm)


ATTN_GROUP = 4


def _attn_kernel(q_ref, k_ref, v_ref, nu_ref, o_ref, qs_ref, z0_ref, z1_ref, sp0_ref, sp1_ref,
                 w0_ref, w1_ref, acc_ref, carry_ref, *, tq):
    z_refs, sp_refs, w_refs = (z0_ref, z1_ref), (sp0_ref, sp1_ref), (w0_ref, w1_ref)
    i = pl.program_id(2)
    hd = C_HEAD_DIM
    g = ATTN_GROUP
    width = g * hd
    q = q_ref[...]
    lane = lax.broadcasted_iota(I32, (tq, width), 1)
    for h in range(g):
        in_head = (lane >= h * hd) & (lane < (h + 1) * hd)
        qs_ref[h * tq:(h + 1) * tq, :] = jnp.where(in_head, q, jnp.zeros_like(q))
    neg_u = nu_ref[...]
    row = lax.broadcasted_iota(I32, (g * tq, tq), 0) & (tq - 1)
    strict = lax.broadcasted_iota(I32, (g * tq, tq), 1) < row
    vmask = [(lane >= h * hd) & (lane < (h + 1) * hd) for h in range(g)]

    def key_rows(n):
        return pl.ds(pl.multiple_of(jnp.maximum(i - n, 0) * tq, tq), tq)

    def scores(n, slot, diag):
        z = _dot_nt(qs_ref[...], k_ref[key_rows(n), :])
        z_refs[slot][...] = z
        zb = z.astype(BF16)
        sp = jnp.maximum(zb, 0) + jnp.log(1 + jnp.exp(-jnp.abs(zb)))
        if diag:
            sp = jnp.where(strict, sp, jnp.zeros_like(sp))
        sp_refs[slot][...] = sp

    def weights(slot, diag):
        suf = _dot(sp_refs[slot][...], neg_u)
        w = jnp.exp(z_refs[slot][...] + suf + carry_ref[...])
        if diag:
            w = jnp.where(strict, w, 0.0)
        w = w.astype(BF16)
        for h in range(g):
            w_refs[slot][:, h * tq:(h + 1) * tq] = w[h * tq:(h + 1) * tq, :]
        carry_ref[...] = carry_ref[...] + suf[:, 0:1]

    def output(n, slot, valid=None):
        vj = v_ref[key_rows(n), :]
        v_cat = jnp.concatenate([jnp.where(m, vj, jnp.zeros_like(vj)) for m in vmask], axis=0)
        pv = _dot(w_refs[slot][...], v_cat)
        if valid is not None:
            pv = jnp.where(valid, pv, 0.0)
        acc_ref[...] = acc_ref[...] + pv

    carry_ref[...] = jnp.zeros_like(carry_ref)
    acc_ref[...] = jnp.zeros_like(acc_ref)
    scores(0, 0, True)
    weights(0, True)
    scores(1, 1, False)

    def body(p, carry):
        t = 2 + 2 * p
        output(t - 2, 0)
        weights(1, False)
        scores(t, 0, False)
        output(t - 1, 1, valid=(t - 1 <= i))
        weights(0, False)
        scores(t + 1, 1, False)
        return carry

    lax.fori_loop(0, (i + 2) // 2, body, 0)
    o_ref[...] = acc_ref[...].astype(o_ref.dtype)


def _attention(qkv, bsz, seq, width, tq=256):
    tq = min(tq, seq)
    blk = ATTN_GROUP * C_HEAD_DIM
    ngroups = width // blk
    nq = seq // tq
    neg_u = -(jnp.arange(tq)[:, None] >= jnp.arange(tq)[None, :]).astype(BF16)
    qkv3 = qkv.reshape(bsz, seq, 3 * width)
    return pl.pallas_call(
        functools.partial(_attn_kernel, tq=tq),
        grid=(bsz, ngroups, nq),
        in_specs=[pl.BlockSpec((None, tq, blk), lambda b, p, i: (b, i, p)),
                  pl.BlockSpec((None, seq, blk), lambda b, p, i: (b, 0, ngroups + p)),
                  pl.BlockSpec((None, seq, blk), lambda b, p, i: (b, 0, 2 * ngroups + p)),
                  pl.BlockSpec((tq, tq), lambda b, p, i: (0, 0))],
        out_specs=pl.BlockSpec((None, tq, blk), lambda b, p, i: (b, i, p)),
        out_shape=jax.ShapeDtypeStruct((bsz, seq, width), BF16),
        scratch_shapes=[pltpu.VMEM((ATTN_GROUP * tq, blk), BF16),
                        pltpu.VMEM((ATTN_GROUP * tq, tq), F32),
                        pltpu.VMEM((ATTN_GROUP * tq, tq), F32),
                        pltpu.VMEM((ATTN_GROUP * tq, tq), BF16),
                        pltpu.VMEM((ATTN_GROUP * tq, tq), BF16),
                        pltpu.VMEM((tq, ATTN_GROUP * tq), BF16),
                        pltpu.VMEM((tq, ATTN_GROUP * tq), BF16),
                        pltpu.VMEM((tq, blk), F32),
                        pltpu.VMEM((ATTN_GROUP * tq, 1), F32)],
        compiler_params=_params("arbitrary", "arbitrary", "arbitrary"),
        name="stickbreak_attn",
    )(qkv3, qkv3, qkv3, neg_u).reshape(bsz * seq, width)


def kernel(x, hgrn_lb_logits, l0_w_in, l0_a_norm_w, l0_w_pool, l0_pool_scale, l0_w_o, l0_ln1_w, l0_ln1_b, l0_w_router, l0_b_router, l0_w1, l0_b1, l0_w2, l0_b2, l0_ln2_w, l0_ln2_b, l1_w_in, l1_w_o, l1_ln1_w, l1_ln1_b, l1_w_router, l1_b_router, l1_w1, l1_b1, l1_w2, l1_b2, l1_ln2_w, l1_ln2_b):
    bsz, seq, d = x.shape
    out_dtype = x.dtype
    x2d = x.reshape(bsz * seq, d).astype(F32)
    r1 = lambda a: a.reshape(1, -1).astype(F32)

    a_width = A_HEADS * A_DK
    lb = jnp.cumsum(jax.nn.softmax(hgrn_lb_logits.astype(F32), axis=0), axis=0)[0]
    proj0 = _proj(x2d, l0_w_in.astype(BF16), F32)
    a_out = _hgrn(proj0, r1(lb), r1(l0_a_norm_w), bsz, seq)
    b_width = l0_w_pool.shape[0] * l0_w_pool.shape[1]
    b_out = _pool(proj0, l0_w_pool.astype(BF16), r1(l0_pool_scale), bsz, seq,
                  col_block=(4 * a_width) // b_width)
    x1, idx, gates, rank, counts = _mix_router(
        [a_out, b_out], l0_w_o.astype(BF16), x2d, r1(l0_ln1_w), r1(l0_ln1_b),
        l0_w_router.astype(F32), r1(l0_b_router))
    x2d = _moe(x1, idx, gates, rank, counts, l0_w1, l0_b1, l0_w2, l0_b2,
               r1(l0_ln2_w), r1(l0_ln2_b))

    c_width = l1_w_in.shape[1] // 3
    qscale = jnp.concatenate([jnp.full((c_width,), C_HEAD_DIM ** -0.5, F32),
                              jnp.ones((2 * c_width,), F32)])
    qkv = _proj(x2d, (l1_w_in * qscale).astype(BF16), BF16)
    o = _attention(qkv, bsz, seq, c_width)
    x1, idx, gates, rank, counts = _mix_router(
        [o], l1_w_o.astype(BF16), x2d, r1(l1_ln1_w), r1(l1_ln1_b),
        l1_w_router.astype(F32), r1(l1_b_router))
    x2d = _moe(x1, idx, gates, rank, counts, l1_w1, l1_b1, l1_w2, l1_b2,
               r1(l1_ln2_w), r1(l1_ln2_b))
    return x2d.reshape(bsz, seq, d).astype(out_dtype)
```

```python
import functools

import jax
import jax.numpy as jnp
from jax import lax
from jax.experimental import pallas as pl
from jax.experimental.pallas import tpu as pltpu

F32 = jnp.float32
BF16 = jnp.bfloat16
I32 = jnp.int32

DEPTH = 2
DEEPNORM_ALPHA = (2.0 * DEPTH) ** 0.25
LN_EPS = 1e-5
RMS_EPS = 1e-6

A_HEADS = 4
A_DK = 128
POOL_WINDOWS = (2, 4, 8, 16)
POOL_HALO = 16
C_HEAD_DIM = 64
N_EXPERTS = 32
TOP_K = 4
SWIGLU_ALPHA = 1.702
SWIGLU_LIMIT = 7.0

HGRN_CHUNK = 16
ROW_DMA_UNROLL = 8
VMEM_LIMIT = 48 * 1024 * 1024


def _params(*sem):
    return pltpu.CompilerParams(dimension_semantics=sem, vmem_limit_bytes=VMEM_LIMIT)


def _dot(a, b):
    return jnp.dot(a, b, preferred_element_type=F32)


def _dot_nt(a, b):
    return lax.dot_general(a, b, (((1,), (1,)), ((), ())), preferred_element_type=F32)


def _dot_tn(a, b):
    return lax.dot_general(a, b, (((0,), (0,)), ((), ())), preferred_element_type=F32)


def _sigmoid(x):
    e = jnp.exp(-jnp.abs(x))
    r = 1.0 / (1.0 + e)
    return jnp.where(x >= 0, r, e * r)


def _layer_norm(y, w, b):
    mu = jnp.mean(y, axis=-1, keepdims=True)
    yc = y - mu
    var = jnp.mean(yc * yc, axis=-1, keepdims=True)
    return yc * lax.rsqrt(var + LN_EPS) * w + b


def _proj_kernel(x_ref, w_ref, o_ref):
    o_ref[...] = _dot(x_ref[...].astype(BF16), w_ref[...]).astype(o_ref.dtype)


def _proj(x2d, w, out_dtype, tm=512):
    m, k = x2d.shape
    n = w.shape[1]
    tm = min(tm, m)
    return pl.pallas_call(
        _proj_kernel,
        grid=(m // tm,),
        in_specs=[pl.BlockSpec((tm, k), lambda i: (i, 0)),
                  pl.BlockSpec((k, n), lambda i: (0, 0))],
        out_specs=pl.BlockSpec((tm, n), lambda i: (i, 0)),
        out_shape=jax.ShapeDtypeStruct((m, n), out_dtype),
        compiler_params=_params("arbitrary"),
        name="proj",
    )(x2d, w)


def _hgrn_kernel(q_ref, f_ref, i_ref, g_ref, lb_ref, nw_ref, ltri_ref, o_ref,
                 st_ref, cum_ref, kk_ref, oacc_ref, *, ts):
    c = HGRN_CHUNK

    @pl.when(pl.program_id(1) == 0)
    def _():
        st_ref[...] = jnp.zeros_like(st_ref)

    z = f_ref[...]
    lb = lb_ref[...]
    e = jnp.exp(-jnp.abs(z))
    r = 1.0 / (1.0 + e)
    er = e * r
    sig = jnp.where(z >= 0, r, er)
    sig_neg = jnp.where(z >= 0, er, r)
    logf = jnp.log(lb + (1.0 - lb) * sig)
    kk_ref[...] = (1.0 - lb) * sig_neg

    ltri = ltri_ref[...]
    x1 = logf.astype(BF16)
    r1 = logf - x1.astype(F32)
    x2 = r1.astype(BF16)
    x3 = (r1 - x2.astype(F32)).astype(BF16)
    cum_ref[...] = _dot(ltri, x1) + _dot(ltri, x2) + _dot(ltri, x3)

    t_iota = lax.broadcasted_iota(I32, (c, A_DK), 0)

    def chunk(ci, carry):
        r0 = pl.multiple_of(ci * c, c)
        for h in range(A_HEADS):
            cols = slice(h * A_DK, (h + 1) * A_DK)
            cum = cum_ref[pl.ds(r0, c), cols]
            q = q_ref[pl.ds(r0, c), cols]
            k = kk_ref[pl.ds(r0, c), cols]
            v = i_ref[pl.ds(r0, c), cols]
            last = cum[c - 1:c, :]
            st = st_ref[h]
            qd = (q * jnp.exp(cum)).astype(BF16)
            o = _dot_nt(qd, st.astype(BF16))
            for s in range(c):
                d = jnp.where(t_iota >= s, cum - cum[s:s + 1, :], -1e30)
                p = q * (k[s:s + 1, :] * jnp.exp(d))
                o = o + jnp.sum(p, axis=-1, keepdims=True) * v[s:s + 1, :]
            oacc_ref[pl.ds(r0, c), cols] = o
            kd = (k * jnp.exp(last - cum)).astype(BF16)
            st_ref[h] = jnp.exp(last) * st + _dot_tn(v.astype(BF16), kd)
        return carry

    lax.fori_loop(0, ts // c, chunk, 0)

    g = g_ref[...]
    gate = g * _sigmoid(g)
    for h in range(A_HEADS):
        cols = slice(h * A_DK, (h + 1) * A_DK)
        o = oacc_ref[:, cols]
        o = o * lax.rsqrt(jnp.mean(o * o, axis=-1, keepdims=True) + RMS_EPS) * nw_ref[...]
        o_ref[:, cols] = o * gate[:, cols]


def _hgrn(proj, lb, norm_w, bsz, seq, ts=256):
    ts = min(ts, seq)
    ns = seq // ts
    h = A_HEADS
    blk = jnp.arange(ts) // HGRN_CHUNK
    ltri = ((blk[:, None] == blk[None, :]) &
            (jnp.arange(ts)[:, None] >= jnp.arange(ts)[None, :])).astype(BF16)

    wd = h * A_DK

    def col(part):
        return pl.BlockSpec((ts, wd), lambda b, s, part=part: (b * ns + s, part))

    return pl.pallas_call(
        functools.partial(_hgrn_kernel, ts=ts),
        grid=(bsz, ns),
        in_specs=[col(0), col(1), col(2), col(3),
                  pl.BlockSpec((1, wd), lambda b, s: (0, 0)),
                  pl.BlockSpec((1, A_DK), lambda b, s: (0, 0)),
                  pl.BlockSpec((ts, ts), lambda b, s: (0, 0))],
        out_specs=pl.BlockSpec((ts, wd), lambda b, s: (b * ns + s, 0)),
        out_shape=jax.ShapeDtypeStruct((bsz * seq, wd), F32),
        scratch_shapes=[pltpu.VMEM((h, A_DK, A_DK), F32),
                        pltpu.VMEM((ts, wd), F32),
                        pltpu.VMEM((ts, wd), F32),
                        pltpu.VMEM((ts, wd), F32)],
        compiler_params=_params("arbitrary", "arbitrary"),
        name="hgrn2",
    )(proj, proj, proj, proj, lb, norm_w, ltri)


def _pool_kernel(v_ref, wp_ref, sc_ref, o_ref, buf_ref, *, ts):
    s = pl.program_id(1)
    hl = POOL_HALO

    @pl.when(s == 0)
    def _():
        buf_ref[0:hl, :] = jnp.zeros((hl, buf_ref.shape[1]), F32)

    x = v_ref[...]
    buf_ref[hl:hl + ts, :] = x
    pos = (s * ts + 1 + lax.broadcasted_iota(I32, (ts, 1), 0)).astype(F32)
    for g, win in enumerate(POOL_WINDOWS):
        lo, hi = g * 128, (g + 1) * 128
        xg = x[:, lo:hi]
        acc = xg
        for j in range(1, win):
            acc = acc + buf_ref[hl - j:hl - j + ts, lo:hi]
        pooled = acc / jnp.minimum(pos, float(win)) - xg
        o_ref[:, lo:hi] = _dot(pooled.astype(BF16), wp_ref[g]) * sc_ref[:, lo:hi]
    buf_ref[0:hl, :] = x[ts - hl:ts, :]


def _pool(proj, w_pool, pool_scale, bsz, seq, col_block, ts=512):
    ts = min(ts, seq)
    ns = seq // ts
    width = w_pool.shape[0] * w_pool.shape[1]
    return pl.pallas_call(
        functools.partial(_pool_kernel, ts=ts),
        grid=(bsz, ns),
        in_specs=[pl.BlockSpec((ts, width), lambda b, s: (b * ns + s, col_block)),
                  pl.BlockSpec(w_pool.shape, lambda b, s: (0, 0, 0)),
                  pl.BlockSpec((1, width), lambda b, s: (0, 0))],
        out_specs=pl.BlockSpec((ts, width), lambda b, s: (b * ns + s, 0)),
        out_shape=jax.ShapeDtypeStruct((bsz * seq, width), F32),
        scratch_shapes=[pltpu.VMEM((POOL_HALO + ts, width), F32)],
        compiler_params=_params("arbitrary", "arbitrary"),
        name="pool",
    )(proj, w_pool, pool_scale)


def _mix_router_kernel(*refs, n_parts, tm):
    parts = refs[:n_parts]
    wos = refs[n_parts:2 * n_parts]
    x_ref, lnw_ref, lnb_ref, wr_ref, br_ref, tri_ref = refs[2 * n_parts:2 * n_parts + 6]
    x1_ref, idx_ref, gate_ref, rank_ref, cnt_ref, run_ref = refs[2 * n_parts + 6:]

    @pl.when(pl.program_id(0) == 0)
    def _():
        run_ref[...] = jnp.zeros_like(run_ref)

    h = _dot(parts[0][...].astype(BF16), wos[0][...])
    for p_ref, w_ref in zip(parts[1:], wos[1:]):
        h = h + _dot(p_ref[...].astype(BF16), w_ref[...])
    x1 = _layer_norm(DEEPNORM_ALPHA * x_ref[...] + h, lnw_ref[...], lnb_ref[...])
    x1_ref[...] = x1

    ne = br_ref.shape[0]
    x_hi = x1.astype(BF16)
    x_mid = (x1 - x_hi.astype(F32)).astype(BF16)
    hi_mid = _dot_nt(wr_ref[...], x_hi)
    logits = hi_mid[:ne] + hi_mid[ne:] + _dot_nt(wr_ref[:ne, :], x_mid) + br_ref[...]
    e_iota = lax.broadcasted_iota(I32, (ne, tm), 0).astype(F32)
    k_iota = lax.broadcasted_iota(I32, (TOP_K, tm), 0)
    l = logits
    vals, idxs = [], []
    for _ in range(TOP_K):
        m = jnp.max(l, axis=0, keepdims=True)
        ik = jnp.min(jnp.where(l == m, e_iota, float(ne)), axis=0, keepdims=True)
        vals.append(m)
        idxs.append(ik)
        l = jnp.where(e_iota == ik, -jnp.inf, l)
    exps = [jnp.exp(v - vals[0]) for v in vals]
    tot = exps[0] + exps[1] + exps[2] + exps[3]

    onehot = jnp.zeros((ne, tm), F32)
    for ik in idxs:
        onehot = onehot + (e_iota == ik).astype(F32)
    base = _dot(onehot.astype(BF16), tri_ref[...]) + run_ref[...]

    idx_out = jnp.zeros((TOP_K, tm), I32)
    gate_out = jnp.zeros((TOP_K, tm), F32)
    rank_out = jnp.zeros((TOP_K, tm), I32)
    for kk in range(TOP_K):
        rk = jnp.sum(jnp.where(e_iota == idxs[kk], base, 0.0), axis=0, keepdims=True)
        idx_out = jnp.where(k_iota == kk, idxs[kk].astype(I32), idx_out)
        gate_out = jnp.where(k_iota == kk, exps[kk] / tot, gate_out)
        rank_out = jnp.where(k_iota == kk, rk.astype(I32), rank_out)
    idx_ref[...] = idx_out
    gate_ref[...] = gate_out
    rank_ref[...] = rank_out
    run_ref[...] = run_ref[...] + jnp.sum(onehot, axis=1, keepdims=True)
    cnt_ref[...] = run_ref[...].astype(I32)


def _mix_router(parts, w_o, x2d, ln_w, ln_b, w_router, b_router, tm=256):
    t, d = x2d.shape
    tm = min(tm, t)
    n_parts = len(parts)
    wos, off = [], 0
    for p in parts:
        wos.append(lax.slice_in_dim(w_o, off, off + p.shape[1], axis=0))
        off += p.shape[1]
    tri = (jnp.arange(tm)[:, None] < jnp.arange(tm)[None, :]).astype(BF16)
    ne = w_router.shape[1]
    row = lambda i: (i, 0)
    col = lambda i: (0, i)
    fixed = lambda i: (0, 0)
    in_specs = ([pl.BlockSpec((tm, p.shape[1]), row) for p in parts] +
                [pl.BlockSpec(w.shape, fixed) for w in wos] +
                [pl.BlockSpec((tm, d), row),
                 pl.BlockSpec((1, d), fixed), pl.BlockSpec((1, d), fixed),
                 pl.BlockSpec((2 * ne, d), fixed), pl.BlockSpec((ne, 1), fixed),
                 pl.BlockSpec((tm, tm), fixed)])
    wr_t = w_router.T
    wr_hi = wr_t.astype(BF16)
    wr_mid = (wr_t - wr_hi.astype(F32)).astype(BF16)
    wr_split = jnp.concatenate([wr_hi, wr_mid], axis=0)
    out_specs = [pl.BlockSpec((tm, d), row),
                 pl.BlockSpec((TOP_K, tm), col),
                 pl.BlockSpec((TOP_K, tm), col),
                 pl.BlockSpec((TOP_K, tm), col),
                 pl.BlockSpec((ne, 1), fixed)]
    out_shape = [jax.ShapeDtypeStruct((t, d), F32),
                 jax.ShapeDtypeStruct((TOP_K, t), I32),
                 jax.ShapeDtypeStruct((TOP_K, t), F32),
                 jax.ShapeDtypeStruct((TOP_K, t), I32),
                 jax.ShapeDtypeStruct((ne, 1), I32)]
    return pl.pallas_call(
        functools.partial(_mix_router_kernel, n_parts=n_parts, tm=tm),
        grid=(t // tm,),
        in_specs=in_specs,
        out_specs=out_specs,
        out_shape=out_shape,
        scratch_shapes=[pltpu.VMEM((ne, 1), F32)],
        compiler_params=_params("arbitrary"),
        name="mix_router",
    )(*parts, *wos, x2d, ln_w, ln_b, wr_split, b_router.reshape(ne, 1), tri)


ROW_SUB = 8
ROW_LANE = 128


def _store_rows_as_tiles(ref, x, n):
    for s in range(ROW_SUB):
        ref[pl.ds(s, n, stride=ROW_SUB), :] = x[:, s * ROW_LANE:(s + 1) * ROW_LANE]


def _load_tiles_as_rows(ref, n):
    return jnp.concatenate([ref[pl.ds(s, n, stride=ROW_SUB), :] for s in range(ROW_SUB)], axis=1)


def _tile_of_row(ref, r):
    return ref.at[pl.ds(pl.multiple_of(r * ROW_SUB, ROW_SUB), ROW_SUB)]


def _dispatch_kernel(starts_ref, pc_ref, pos_ref, x_ref, xs_ref, zero_ref, stage_ref, sem, zsem,
                     *, tm, te):
    ne = starts_ref.shape[0]
    n_tiles = xs_ref.shape[0] // (te * ROW_SUB)
    step = pl.program_id(0)
    slot = step % 2

    def zero_tile(tile):
        rows = te * ROW_SUB
        return pltpu.make_async_copy(
            zero_ref, xs_ref.at[pl.ds(pl.multiple_of(tile * rows, rows), rows)], zsem)

    def wait_rows():
        for kk in range(TOP_K):
            pltpu.make_async_copy(stage_ref.at[0], xs_ref.at[pl.ds(0, tm * ROW_SUB)], sem).wait()

    def last_tile(e):
        return (starts_ref[e] + pc_ref[e]) // te - 1

    @pl.when(pl.program_id(0) == 0)
    def _():
        zero_ref[...] = jnp.zeros_like(zero_ref)
        n_used = (starts_ref[ne - 1] + pc_ref[ne - 1]) // te

        def start(e, carry):
            @pl.when(pc_ref[e] > 0)
            def _():
                zero_tile(last_tile(e)).start()
            return carry

        def wait(e, carry):
            @pl.when(pc_ref[e] > 0)
            def _():
                zero_tile(last_tile(e)).wait()
            return carry

        def start_tail(tile, carry):
            zero_tile(tile).start()
            return carry

        def wait_tail(tile, carry):
            zero_tile(tile).wait()
            return carry

        lax.fori_loop(0, ne, start, 0)
        lax.fori_loop(n_used, n_tiles, start_tail, 0)
        lax.fori_loop(0, ne, wait, 0)
        lax.fori_loop(n_used, n_tiles, wait_tail, 0)

    stage = stage_ref.at[slot]
    _store_rows_as_tiles(stage, x_ref[...], tm)

    @pl.when(step > 0)
    def _():
        wait_rows()

    def row(r, carry):
        for kk in range(TOP_K):
            p = pos_ref[0, 0, kk * tm + r]
            pltpu.make_async_copy(_tile_of_row(stage, r), _tile_of_row(xs_ref, p),
                                  sem).start(priority=kk % 2)
        return carry

    lax.fori_loop(0, tm, row, 0, unroll=ROW_DMA_UNROLL)

    @pl.when(step == pl.num_programs(0) - 1)
    def _():
        wait_rows()


def _dispatch(x1, pos_tiles, starts, pc, n_slots, tm, te):
    t, d = x1.shape
    assert d == ROW_SUB * ROW_LANE
    grid_spec = pltpu.PrefetchScalarGridSpec(
        num_scalar_prefetch=2,
        grid=(t // tm,),
        in_specs=[pl.BlockSpec((1, 1, tm * TOP_K), lambda i, *_: (i, 0, 0),
                               memory_space=pltpu.SMEM),
                  pl.BlockSpec((tm, d), lambda i, *_: (i, 0))],
        out_specs=pl.BlockSpec(memory_space=pl.ANY),
        scratch_shapes=[pltpu.VMEM((te * ROW_SUB, ROW_LANE), F32),
                        pltpu.VMEM((2, tm * ROW_SUB, ROW_LANE), F32),
                        pltpu.SemaphoreType.DMA,
                        pltpu.SemaphoreType.DMA],
    )
    return pl.pallas_call(
        functools.partial(_dispatch_kernel, tm=tm, te=te),
        grid_spec=grid_spec,
        out_shape=jax.ShapeDtypeStruct((n_slots * ROW_SUB, ROW_LANE), F32),
        compiler_params=_params("arbitrary"),
        name="moe_dispatch",
    )(starts, pc, pos_tiles, x1)


DEINT_CHUNK = 512


def _deinterleave_kernel(w_ref, p_ref, g_ref, l_ref):
    half = DEINT_CHUNK // 2
    r = _dot(w_ref[0].astype(BF16), p_ref[...])
    g_ref[0] = r[:, :half].astype(BF16)
    l_ref[0] = r[:, half:].astype(BF16)


def _deinterleave(w1):
    ne, d, two_dff = w1.shape
    c = DEINT_CHUNK
    half = c // 2
    j = jnp.arange(c)
    perm = (j[None, :] == jnp.where(j % 2 == 0, j // 2, half + j // 2)[:, None]).astype(BF16)
    out = jax.ShapeDtypeStruct((ne, d, two_dff // 2), BF16)
    return pl.pallas_call(
        _deinterleave_kernel,
        grid=(ne, two_dff // c),
        in_specs=[pl.BlockSpec((1, d, c), lambda e, k: (e, 0, k)),
                  pl.BlockSpec((c, c), lambda e, k: (0, 0))],
        out_specs=[pl.BlockSpec((1, d, half), lambda e, k: (e, 0, k)),
                   pl.BlockSpec((1, d, half), lambda e, k: (e, 0, k))],
        out_shape=[out, out],
        compiler_params=_params("arbitrary", "arbitrary"),
        name="w1_deinterleave",
    )(w1, perm)


def _expert_kernel(te_ref, nu_ref, x_ref, w1g_ref, w1l_ref, b1g_ref, b1l_ref, w2_ref, b2_ref,
                   o_ref, *, te):
    @pl.when(pl.program_id(0) < nu_ref[0])
    def _():
        xb = _load_tiles_as_rows(x_ref, te).astype(BF16)
        hg = _dot(xb, w1g_ref[0]) + b1g_ref[0]
        hl = _dot(xb, w1l_ref[0]) + b1l_ref[0]
        glu = jnp.minimum(hg, SWIGLU_LIMIT)
        lin = jnp.clip(hl, -SWIGLU_LIMIT, SWIGLU_LIMIT)
        act = glu * _sigmoid(SWIGLU_ALPHA * glu) * (lin + 1.0)
        _store_rows_as_tiles(o_ref, _dot(act.astype(BF16), w2_ref[0]) + b2_ref[0], te)

    @pl.when(pl.program_id(0) >= nu_ref[0])
    def _():
        o_ref[...] = jnp.zeros_like(o_ref)


def _experts(xs, tile_expert, n_used, w1g, w1l, b1g, b1l, w2, b2, te):
    d = w1g.shape[1]
    dff = w1g.shape[2]
    n_tiles = xs.shape[0] // (te * ROW_SUB)
    blk = (te * ROW_SUB, ROW_LANE)

    def row(i, te_ref, nu_ref):
        return (jnp.minimum(i, nu_ref[0] - 1), 0)

    def wsel(i, te_ref, nu_ref):
        return (te_ref[i], 0, 0)

    grid_spec = pltpu.PrefetchScalarGridSpec(
        num_scalar_prefetch=2,
        grid=(n_tiles,),
        in_specs=[pl.BlockSpec(blk, row),
                  pl.BlockSpec((1, d, dff), wsel), pl.BlockSpec((1, d, dff), wsel),
                  pl.BlockSpec((1, 1, dff), wsel), pl.BlockSpec((1, 1, dff), wsel),
                  pl.BlockSpec((1, dff, d), wsel), pl.BlockSpec((1, 1, d), wsel)],
        out_specs=pl.BlockSpec(blk, lambda i, te_ref, nu_ref: (i, 0)),
    )
    return pl.pallas_call(
        functools.partial(_expert_kernel, te=te),
        grid_spec=grid_spec,
        out_shape=jax.ShapeDtypeStruct(xs.shape, F32),
        compiler_params=_params("arbitrary"),
        name="moe_experts",
    )(tile_expert, n_used, xs, w1g, w1l, b1g, b1l, w2, b2)


def _combine_kernel(pos_ref, pos_next_ref, x_ref, gate_ref, lnw_ref, lnb_ref, eo_ref, o_ref,
                    buf_ref, sem, *, tm):
    step = pl.program_id(0)
    slot = step % 2

    def gather(p_ref, slot_):
        def row(r, carry):
            for kk in range(TOP_K):
                p = p_ref[0, 0, kk * tm + r]
                pltpu.make_async_copy(_tile_of_row(eo_ref, p),
                                      _tile_of_row(buf_ref.at[slot_, kk], r),
                                      sem.at[slot_]).start(priority=kk % 2)
            return carry

        lax.fori_loop(0, tm, row, 0, unroll=ROW_DMA_UNROLL)

    @pl.when(step == 0)
    def _():
        gather(pos_ref, slot)

    @pl.when(step + 1 < pl.num_programs(0))
    def _():
        gather(pos_next_ref, 1 - slot)

    for kk in range(TOP_K):
        pltpu.make_async_copy(eo_ref.at[pl.ds(0, tm * ROW_SUB)], buf_ref.at[slot, kk],
                              sem.at[slot]).wait()

    g = gate_ref[...]
    y = g[:, 0:1] * _load_tiles_as_rows(buf_ref.at[slot, 0], tm)
    for kk in range(1, TOP_K):
        y = y + g[:, kk:kk + 1] * _load_tiles_as_rows(buf_ref.at[slot, kk], tm)
    o_ref[...] = _layer_norm(DEEPNORM_ALPHA * x_ref[...] + y, lnw_ref[...], lnb_ref[...])


def _combine(eo, pos_tiles, x1, gates, ln_w, ln_b, tm):
    t, d = x1.shape
    nt = t // tm
    row = lambda i: (i, 0)
    fixed = lambda i: (0, 0)
    pos_spec = lambda f: pl.BlockSpec((1, 1, tm * TOP_K), f, memory_space=pltpu.SMEM)
    return pl.pallas_call(
        functools.partial(_combine_kernel, tm=tm),
        grid=(nt,),
        in_specs=[pos_spec(lambda i: (i, 0, 0)),
                  pos_spec(lambda i: (jnp.minimum(i + 1, nt - 1), 0, 0)),
                  pl.BlockSpec((tm, d), row),
                  pl.BlockSpec((tm, TOP_K), row),
                  pl.BlockSpec((1, d), fixed), pl.BlockSpec((1, d), fixed),
                  pl.BlockSpec(memory_space=pl.ANY)],
        out_specs=pl.BlockSpec((tm, d), row),
        out_shape=jax.ShapeDtypeStruct((t, d), F32),
        scratch_shapes=[pltpu.VMEM((2, TOP_K, tm * ROW_SUB, ROW_LANE), F32),
                        pltpu.SemaphoreType.DMA((2,))],
        compiler_params=_params("arbitrary"),
        name="moe_combine",
    )(pos_tiles, pos_tiles, x1, gates, ln_w, ln_b, eo)


def _moe(x1, idx, gates, rank, counts, w1, b1, w2, b2, ln_w, ln_b, tm=256, te=256):
    t, d = x1.shape
    tm = min(tm, t)
    ne = w1.shape[0]
    counts = counts[:, 0]
    pc = ((counts + te - 1) // te) * te
    ends = jnp.cumsum(pc)
    starts = ends - pc
    n_tiles = (t * TOP_K) // te + ne
    n_used = (ends[-1] // te).astype(I32)
    tile_ids = jnp.arange(n_tiles, dtype=I32)
    te_raw = jnp.sum((tile_ids[:, None] >= (ends // te)[None, :]).astype(I32), axis=1)
    te_raw = jnp.minimum(te_raw, ne - 1)
    tile_expert = jnp.where(tile_ids < n_used, te_raw, jnp.take(te_raw, n_used - 1))
    e_ids = jnp.arange(ne, dtype=I32)[:, None, None]
    start_of = jnp.sum(jnp.where(idx[None] == e_ids, starts.astype(I32)[:, None, None], 0), axis=0)
    pos = start_of + rank
    pos_tiles = pos.reshape(TOP_K, t // tm, tm).transpose(1, 0, 2).reshape(t // tm, 1, TOP_K * tm)
    gates = gates.T

    w1g, w1l = _deinterleave(w1)
    b1g = b1[:, None, 0::2]
    b1l = b1[:, None, 1::2]

    xs = _dispatch(x1, pos_tiles, starts.astype(I32), pc.astype(I32), n_tiles * te, tm, te)
    eo = _experts(xs, tile_expert, n_used.reshape(1), w1g, w1l, b1g, b1l,
                  w2.astype(BF16), b2[:, None, :], te)
    return _combine(eo, pos_tiles, x1, gates, ln_w, ln_b, tm)


ATTN_GROUP = 4
ATTN_UNROLL = 4


def _attn_kernel(q_ref, k_ref, vt_ref, nu_ref, o_ref, qs_ref, z0_ref, z1_ref, sp0_ref, sp1_ref,
                 w0_ref, w1_ref, acc_ref, carry_ref, *, tq):
    z_refs, sp_refs, w_refs = (z0_ref, z1_ref), (sp0_ref, sp1_ref), (w0_ref, w1_ref)
    i = pl.program_id(2)
    hd = C_HEAD_DIM
    g = ATTN_GROUP
    width = g * hd
    q = q_ref[...]
    lane = lax.broadcasted_iota(I32, (tq, width), 1)
    for h in range(g):
        in_head = (lane >= h * hd) & (lane < (h + 1) * hd)
        qs_ref[h * tq:(h + 1) * tq, :] = jnp.where(in_head, q, jnp.zeros_like(q))
    neg_u = nu_ref[...]
    row = lax.broadcasted_iota(I32, (g * tq, tq), 0) & (tq - 1)
    strict = lax.broadcasted_iota(I32, (g * tq, tq), 1) < row

    def key_rows(n):
        return pl.ds(pl.multiple_of(jnp.maximum(i - n, 0) * tq, tq), tq)

    def scores(n, slot, diag):
        z = _dot_nt(qs_ref[...], k_ref[key_rows(n), :])
        z_refs[slot][...] = z
        zb = z.astype(BF16)
        sp = jnp.maximum(zb, 0) + jnp.log(1 + jnp.exp(-jnp.abs(zb)))
        if diag:
            sp = jnp.where(strict, sp, jnp.zeros_like(sp))
        sp_refs[slot][...] = sp

    def weights(slot, diag):
        suf = _dot(sp_refs[slot][...], neg_u)
        w = jnp.exp(z_refs[slot][...] + suf + carry_ref[...])
        if diag:
            w = jnp.where(strict, w, 0.0)
        w_refs[slot][...] = w.astype(BF16)
        carry_ref[...] = carry_ref[...] + suf[:, 0:1]

    def output(n, slot, valid):
        vt = vt_ref[jnp.maximum(i - n, 0)]
        for h in range(g):
            pv = _dot_nt(vt[h * hd:(h + 1) * hd, :], w_refs[slot][h * tq:(h + 1) * tq, :])
            if valid is not None:
                pv = jnp.where(valid, pv, 0.0)
            acc_ref[h * hd:(h + 1) * hd, :] = acc_ref[h * hd:(h + 1) * hd, :] + pv

    carry_ref[...] = jnp.zeros_like(carry_ref)
    acc_ref[...] = jnp.zeros_like(acc_ref)
    scores(0, 0, True)
    weights(0, True)
    scores(1, 1, False)

    def body(p, carry):
        for u in range(ATTN_UNROLL):
            t = 2 + ATTN_UNROLL * p + u
            output(t - 2, u % 2, None if u == 0 else (t - 2 <= i))
            weights((u + 1) % 2, False)
            scores(t, u % 2, False)
        return carry

    lax.fori_loop(0, (i + ATTN_UNROLL) // ATTN_UNROLL, body, 0)
    o_ref[...] = acc_ref[...].T.astype(o_ref.dtype)


def _attention(qkv, bsz, seq, width, tq=256):
    tq = min(tq, seq)
    blk = ATTN_GROUP * C_HEAD_DIM
    ngroups = width // blk
    nq = seq // tq
    neg_u = -(jnp.arange(tq)[:, None] >= jnp.arange(tq)[None, :]).astype(BF16)
    qkv3 = qkv.reshape(bsz, seq, 3 * width)
    vt = qkv3[:, :, 2 * width:].reshape(bsz, nq, tq, width).transpose(0, 1, 3, 2)
    return pl.pallas_call(
        functools.partial(_attn_kernel, tq=tq),
        grid=(bsz, ngroups, nq),
        in_specs=[pl.BlockSpec((None, tq, blk), lambda b, p, i: (b, i, p)),
                  pl.BlockSpec((None, seq, blk), lambda b, p, i: (b, 0, ngroups + p)),
                  pl.BlockSpec((None, nq, blk, tq), lambda b, p, i: (b, 0, p, 0)),
                  pl.BlockSpec((tq, tq), lambda b, p, i: (0, 0))],
        out_specs=pl.BlockSpec((None, tq, blk), lambda b, p, i: (b, i, p)),
        out_shape=jax.ShapeDtypeStruct((bsz, seq, width), BF16),
        scratch_shapes=[pltpu.VMEM((ATTN_GROUP * tq, blk), BF16),
                        pltpu.VMEM((ATTN_GROUP * tq, tq), F32),
                        pltpu.VMEM((ATTN_GROUP * tq, tq), F32),
                        pltpu.VMEM((ATTN_GROUP * tq, tq), BF16),
                        pltpu.VMEM((ATTN_GROUP * tq, tq), BF16),
                        pltpu.VMEM((ATTN_GROUP * tq, tq), BF16),
                        pltpu.VMEM((ATTN_GROUP * tq, tq), BF16),
                        pltpu.VMEM((blk, tq), F32),
                        pltpu.VMEM((ATTN_GROUP * tq, 1), F32)],
        compiler_params=_params("arbitrary", "arbitrary", "arbitrary"),
        name="stickbreak_attn",
    )(qkv3, qkv3, vt, neg_u).reshape(bsz * seq, width)


def kernel(x, hgrn_lb_logits, l0_w_in, l0_a_norm_w, l0_w_pool, l0_pool_scale, l0_w_o, l0_ln1_w, l0_ln1_b, l0_w_router, l0_b_router, l0_w1, l0_b1, l0_w2, l0_b2, l0_ln2_w, l0_ln2_b, l1_w_in, l1_w_o, l1_ln1_w, l1_ln1_b, l1_w_router, l1_b_router, l1_w1, l1_b1, l1_w2, l1_b2, l1_ln2_w, l1_ln2_b):
    bsz, seq, d = x.shape
    out_dtype = x.dtype
    x2d = x.reshape(bsz * seq, d).astype(F32)
    r1 = lambda a: a.reshape(1, -1).astype(F32)

    a_width = A_HEADS * A_DK
    lb = jnp.cumsum(jax.nn.softmax(hgrn_lb_logits.astype(F32), axis=0), axis=0)[0]
    proj0 = _proj(x2d, l0_w_in.astype(BF16), F32)
    a_out = _hgrn(proj0, r1(lb), r1(l0_a_norm_w), bsz, seq)
    b_width = l0_w_pool.shape[0] * l0_w_pool.shape[1]
    b_out = _pool(proj0, l0_w_pool.astype(BF16), r1(l0_pool_scale), bsz, seq,
                  col_block=(4 * a_width) // b_width)
    x1, idx, gates, rank, counts = _mix_router(
        [a_out, b_out], l0_w_o.astype(BF16), x2d, r1(l0_ln1_w), r1(l0_ln1_b),
        l0_w_router.astype(F32), r1(l0_b_router))
    x2d = _moe(x1, idx, gates, rank, counts, l0_w1, l0_b1, l0_w2, l0_b2,
               r1(l0_ln2_w), r1(l0_ln2_b))

    c_width = l1_w_in.shape[1] // 3
    qscale = jnp.concatenate([jnp.full((c_width,), C_HEAD_DIM ** -0.5, F32),
                              jnp.ones((2 * c_width,), F32)])
    qkv = _proj(x2d, (l1_w_in * qscale).astype(BF16), BF16)
    o = _attention(qkv, bsz, seq, c_width)
    x1, idx, gates, rank, counts = _mix_router(
        [o], l1_w_o.astype(BF16), x2d, r1(l1_ln1_w), r1(l1_ln1_b),
        l1_w_router.astype(F32), r1(l1_b_router))
    x2d = _moe(x1, idx, gates, rank, counts, l1_w1, l1_b1, l1_w2, l1_b2,
               r1(l1_ln2_w), r1(l1_ln2_b))
    return x2d.reshape(bsz, seq, d).astype(out_dtype)
```

```python
import functools

import jax
import jax.numpy as jnp
from jax import lax
from jax.experimental import pallas as pl
from jax.experimental.pallas import tpu as pltpu

F32 = jnp.float32
BF16 = jnp.bfloat16
I32 = jnp.int32

DEPTH = 2
DEEPNORM_ALPHA = (2.0 * DEPTH) ** 0.25
LN_EPS = 1e-5
RMS_EPS = 1e-6

A_HEADS = 4
A_DK = 128
POOL_WINDOWS = (2, 4, 8, 16)
POOL_HALO = 16
C_HEAD_DIM = 64
N_EXPERTS = 32
TOP_K = 4
SWIGLU_ALPHA = 1.702
SWIGLU_LIMIT = 7.0

HGRN_CHUNK = 16
ROW_DMA_UNROLL = 8
VMEM_LIMIT = 48 * 1024 * 1024


def _params(*sem):
    return pltpu.CompilerParams(dimension_semantics=sem, vmem_limit_bytes=VMEM_LIMIT)


def _dot(a, b):
    return jnp.dot(a, b, preferred_element_type=F32)


def _dot_nt(a, b):
    return lax.dot_general(a, b, (((1,), (1,)), ((), ())), preferred_element_type=F32)


def _dot_tn(a, b):
    return lax.dot_general(a, b, (((0,), (0,)), ((), ())), preferred_element_type=F32)


def _sigmoid(x):
    e = jnp.exp(-jnp.abs(x))
    r = 1.0 / (1.0 + e)
    return jnp.where(x >= 0, r, e * r)


def _layer_norm(y, w, b):
    mu = jnp.mean(y, axis=-1, keepdims=True)
    yc = y - mu
    var = jnp.mean(yc * yc, axis=-1, keepdims=True)
    return yc * lax.rsqrt(var + LN_EPS) * w + b


def _proj_kernel(x_ref, w_ref, o_ref):
    o_ref[...] = _dot(x_ref[...].astype(BF16), w_ref[...]).astype(o_ref.dtype)


def _proj(x2d, w, out_dtype, tm=512):
    m, k = x2d.shape
    n = w.shape[1]
    tm = min(tm, m)
    return pl.pallas_call(
        _proj_kernel,
        grid=(m // tm,),
        in_specs=[pl.BlockSpec((tm, k), lambda i: (i, 0)),
                  pl.BlockSpec((k, n), lambda i: (0, 0))],
        out_specs=pl.BlockSpec((tm, n), lambda i: (i, 0)),
        out_shape=jax.ShapeDtypeStruct((m, n), out_dtype),
        compiler_params=_params("arbitrary"),
        name="proj",
    )(x2d, w)


def _hgrn_kernel(q_ref, f_ref, i_ref, g_ref, lb_ref, nw_ref, ltri_ref, o_ref,
                 st_ref, cum_ref, kk_ref, oacc_ref, *, ts):
    c = HGRN_CHUNK

    @pl.when(pl.program_id(1) == 0)
    def _():
        st_ref[...] = jnp.zeros_like(st_ref)

    z = f_ref[...]
    lb = lb_ref[...]
    e = jnp.exp(-jnp.abs(z))
    r = 1.0 / (1.0 + e)
    er = e * r
    sig = jnp.where(z >= 0, r, er)
    sig_neg = jnp.where(z >= 0, er, r)
    logf = jnp.log(lb + (1.0 - lb) * sig)
    kk_ref[...] = (1.0 - lb) * sig_neg

    ltri = ltri_ref[...]
    x1 = logf.astype(BF16)
    r1 = logf - x1.astype(F32)
    x2 = r1.astype(BF16)
    x3 = (r1 - x2.astype(F32)).astype(BF16)
    cum_ref[...] = _dot(ltri, x1) + _dot(ltri, x2) + _dot(ltri, x3)

    t_iota = lax.broadcasted_iota(I32, (c, A_DK), 0)

    def chunk(ci, carry):
        r0 = pl.multiple_of(ci * c, c)
        for h in range(A_HEADS):
            cols = slice(h * A_DK, (h + 1) * A_DK)
            cum = cum_ref[pl.ds(r0, c), cols]
            q = q_ref[pl.ds(r0, c), cols]
            k = kk_ref[pl.ds(r0, c), cols]
            v = i_ref[pl.ds(r0, c), cols]
            last = cum[c - 1:c, :]
            st = st_ref[h]
            qd = (q * jnp.exp(cum)).astype(BF16)
            o = _dot_nt(qd, st.astype(BF16))
            for s in range(c):
                d = jnp.where(t_iota >= s, cum - cum[s:s + 1, :], -1e30)
                p = q * (k[s:s + 1, :] * jnp.exp(d))
                o = o + jnp.sum(p, axis=-1, keepdims=True) * v[s:s + 1, :]
            oacc_ref[pl.ds(r0, c), cols] = o
            kd = (k * jnp.exp(last - cum)).astype(BF16)
            st_ref[h] = jnp.exp(last) * st + _dot_tn(v.astype(BF16), kd)
        return carry

    lax.fori_loop(0, ts // c, chunk, 0)

    g = g_ref[...]
    gate = g * _sigmoid(g)
    for h in range(A_HEADS):
        cols = slice(h * A_DK, (h + 1) * A_DK)
        o = oacc_ref[:, cols]
        o = o * lax.rsqrt(jnp.mean(o * o, axis=-1, keepdims=True) + RMS_EPS) * nw_ref[...]
        o_ref[:, cols] = o * gate[:, cols]


def _hgrn(proj, lb, norm_w, bsz, seq, ts=256):
    ts = min(ts, seq)
    ns = seq // ts
    h = A_HEADS
    blk = jnp.arange(ts) // HGRN_CHUNK
    ltri = ((blk[:, None] == blk[None, :]) &
            (jnp.arange(ts)[:, None] >= jnp.arange(ts)[None, :])).astype(BF16)

    wd = h * A_DK

    def col(part):
        return pl.BlockSpec((ts, wd), lambda b, s, part=part: (b * ns + s, part))

    return pl.pallas_call(
        functools.partial(_hgrn_kernel, ts=ts),
        grid=(bsz, ns),
        in_specs=[col(0), col(1), col(2), col(3),
                  pl.BlockSpec((1, wd), lambda b, s: (0, 0)),
                  pl.BlockSpec((1, A_DK), lambda b, s: (0, 0)),
                  pl.BlockSpec((ts, ts), lambda b, s: (0, 0))],
        out_specs=pl.BlockSpec((ts, wd), lambda b, s: (b * ns + s, 0)),
        out_shape=jax.ShapeDtypeStruct((bsz * seq, wd), F32),
        scratch_shapes=[pltpu.VMEM((h, A_DK, A_DK), F32),
                        pltpu.VMEM((ts, wd), F32),
                        pltpu.VMEM((ts, wd), F32),
                        pltpu.VMEM((ts, wd), F32)],
        compiler_params=_params("arbitrary", "arbitrary"),
        name="hgrn2",
    )(proj, proj, proj, proj, lb, norm_w, ltri)


def _pool_kernel(v_ref, wp_ref, sc_ref, o_ref, buf_ref, *, ts):
    s = pl.program_id(1)
    hl = POOL_HALO

    @pl.when(s == 0)
    def _():
        buf_ref[0:hl, :] = jnp.zeros((hl, buf_ref.shape[1]), F32)

    x = v_ref[...]
    buf_ref[hl:hl + ts, :] = x
    pos = (s * ts + 1 + lax.broadcasted_iota(I32, (ts, 1), 0)).astype(F32)
    for g, win in enumerate(POOL_WINDOWS):
        lo, hi = g * 128, (g + 1) * 128
        xg = x[:, lo:hi]
        acc = xg
        for j in range(1, win):
            acc = acc + buf_ref[hl - j:hl - j + ts, lo:hi]
        pooled = acc / jnp.minimum(pos, float(win)) - xg
        o_ref[:, lo:hi] = _dot(pooled.astype(BF16), wp_ref[g]) * sc_ref[:, lo:hi]
    buf_ref[0:hl, :] = x[ts - hl:ts, :]


def _pool(proj, w_pool, pool_scale, bsz, seq, col_block, ts=512):
    ts = min(ts, seq)
    ns = seq // ts
    width = w_pool.shape[0] * w_pool.shape[1]
    return pl.pallas_call(
        functools.partial(_pool_kernel, ts=ts),
        grid=(bsz, ns),
        in_specs=[pl.BlockSpec((ts, width), lambda b, s: (b * ns + s, col_block)),
                  pl.BlockSpec(w_pool.shape, lambda b, s: (0, 0, 0)),
                  pl.BlockSpec((1, width), lambda b, s: (0, 0))],
        out_specs=pl.BlockSpec((ts, width), lambda b, s: (b * ns + s, 0)),
        out_shape=jax.ShapeDtypeStruct((bsz * seq, width), F32),
        scratch_shapes=[pltpu.VMEM((POOL_HALO + ts, width), F32)],
        compiler_params=_params("arbitrary", "arbitrary"),
        name="pool",
    )(proj, w_pool, pool_scale)


def _mix_router_kernel(*refs, n_parts, tm):
    parts = refs[:n_parts]
    wos = refs[n_parts:2 * n_parts]
    x_ref, lnw_ref, lnb_ref, wr_ref, br_ref, tri_ref = refs[2 * n_parts:2 * n_parts + 6]
    x1_ref, idx_ref, gate_ref, rank_ref, cnt_ref, run_ref = refs[2 * n_parts + 6:]

    @pl.when(pl.program_id(0) == 0)
    def _():
        run_ref[...] = jnp.zeros_like(run_ref)

    h = _dot(parts[0][...].astype(BF16), wos[0][...])
    for p_ref, w_ref in zip(parts[1:], wos[1:]):
        h = h + _dot(p_ref[...].astype(BF16), w_ref[...])
    x1 = _layer_norm(DEEPNORM_ALPHA * x_ref[...] + h, lnw_ref[...], lnb_ref[...])
    x1_ref[...] = x1

    ne = br_ref.shape[0]
    x_hi = x1.astype(BF16)
    x_mid = (x1 - x_hi.astype(F32)).astype(BF16)
    hi_mid = _dot_nt(wr_ref[...], x_hi)
    logits = hi_mid[:ne] + hi_mid[ne:] + _dot_nt(wr_ref[:ne, :], x_mid) + br_ref[...]
    e_iota = lax.broadcasted_iota(I32, (ne, tm), 0).astype(F32)
    k_iota = lax.broadcasted_iota(I32, (TOP_K, tm), 0)
    l = logits
    vals, idxs = [], []
    for _ in range(TOP_K):
        m = jnp.max(l, axis=0, keepdims=True)
        ik = jnp.min(jnp.where(l == m, e_iota, float(ne)), axis=0, keepdims=True)
        vals.append(m)
        idxs.append(ik)
        l = jnp.where(e_iota == ik, -jnp.inf, l)
    exps = [jnp.exp(v - vals[0]) for v in vals]
    tot = exps[0] + exps[1] + exps[2] + exps[3]

    onehot = jnp.zeros((ne, tm), F32)
    for ik in idxs:
        onehot = onehot + (e_iota == ik).astype(F32)
    base = _dot(onehot.astype(BF16), tri_ref[...]) + run_ref[...]

    idx_out = jnp.zeros((TOP_K, tm), I32)
    gate_out = jnp.zeros((TOP_K, tm), F32)
    rank_out = jnp.zeros((TOP_K, tm), I32)
    for kk in range(TOP_K):
        rk = jnp.sum(jnp.where(e_iota == idxs[kk], base, 0.0), axis=0, keepdims=True)
        idx_out = jnp.where(k_iota == kk, idxs[kk].astype(I32), idx_out)
        gate_out = jnp.where(k_iota == kk, exps[kk] / tot, gate_out)
        rank_out = jnp.where(k_iota == kk, rk.astype(I32), rank_out)
    idx_ref[...] = idx_out
    gate_ref[...] = gate_out
    rank_ref[...] = rank_out
    run_ref[...] = run_ref[...] + jnp.sum(onehot, axis=1, keepdims=True)
    cnt_ref[...] = run_ref[...].astype(I32)


def _mix_router(parts, w_o, x2d, ln_w, ln_b, w_router, b_router, tm=256):
    t, d = x2d.shape
    tm = min(tm, t)
    n_parts = len(parts)
    wos, off = [], 0
    for p in parts:
        wos.append(lax.slice_in_dim(w_o, off, off + p.shape[1], axis=0))
        off += p.shape[1]
    tri = (jnp.arange(tm)[:, None] < jnp.arange(tm)[None, :]).astype(BF16)
    ne = w_router.shape[1]
    row = lambda i: (i, 0)
    col = lambda i: (0, i)
    fixed = lambda i: (0, 0)
    in_specs = ([pl.BlockSpec((tm, p.shape[1]), row) for p in parts] +
                [pl.BlockSpec(w.shape, fixed) for w in wos] +
                [pl.BlockSpec((tm, d), row),
                 pl.BlockSpec((1, d), fixed), pl.BlockSpec((1, d), fixed),
                 pl.BlockSpec((2 * ne, d), fixed), pl.BlockSpec((ne, 1), fixed),
                 pl.BlockSpec((tm, tm), fixed)])
    wr_t = w_router.T
    wr_hi = wr_t.astype(BF16)
    wr_mid = (wr_t - wr_hi.astype(F32)).astype(BF16)
    wr_split = jnp.concatenate([wr_hi, wr_mid], axis=0)
    out_specs = [pl.BlockSpec((tm, d), row),
                 pl.BlockSpec((TOP_K, tm), col),
                 pl.BlockSpec((TOP_K, tm), col),
                 pl.BlockSpec((TOP_K, tm), col),
                 pl.BlockSpec((ne, 1), fixed)]
    out_shape = [jax.ShapeDtypeStruct((t, d), F32),
                 jax.ShapeDtypeStruct((TOP_K, t), I32),
                 jax.ShapeDtypeStruct((TOP_K, t), F32),
                 jax.ShapeDtypeStruct((TOP_K, t), I32),
                 jax.ShapeDtypeStruct((ne, 1), I32)]
    return pl.pallas_call(
        functools.partial(_mix_router_kernel, n_parts=n_parts, tm=tm),
        grid=(t // tm,),
        in_specs=in_specs,
        out_specs=out_specs,
        out_shape=out_shape,
        scratch_shapes=[pltpu.VMEM((ne, 1), F32)],
        compiler_params=_params("arbitrary"),
        name="mix_router",
    )(*parts, *wos, x2d, ln_w, ln_b, wr_split, b_router.reshape(ne, 1), tri)


ROW_SUB = 8
ROW_LANE = 128


def _store_rows_as_tiles(ref, x, n):
    for s in range(ROW_SUB):
        ref[pl.ds(s, n, stride=ROW_SUB), :] = x[:, s * ROW_LANE:(s + 1) * ROW_LANE]


def _load_tiles_as_rows(ref, n):
    return jnp.concatenate([ref[pl.ds(s, n, stride=ROW_SUB), :] for s in range(ROW_SUB)], axis=1)


def _tile_of_row(ref, r):
    return ref.at[pl.ds(pl.multiple_of(r * ROW_SUB, ROW_SUB), ROW_SUB)]


def _dispatch_kernel(starts_ref, pc_ref, pos_ref, x_ref, xs_ref, zero_ref, stage_ref, sem, zsem,
                     *, tm, te):
    ne = starts_ref.shape[0]
    n_tiles = xs_ref.shape[0] // (te * ROW_SUB)
    step = pl.program_id(0)
    slot = step % 2

    def zero_tile(tile):
        rows = te * ROW_SUB
        return pltpu.make_async_copy(
            zero_ref, xs_ref.at[pl.ds(pl.multiple_of(tile * rows, rows), rows)], zsem)

    def wait_rows():
        for kk in range(TOP_K):
            pltpu.make_async_copy(stage_ref.at[0], xs_ref.at[pl.ds(0, tm * ROW_SUB)], sem).wait()

    def last_tile(e):
        return (starts_ref[e] + pc_ref[e]) // te - 1

    @pl.when(pl.program_id(0) == 0)
    def _():
        zero_ref[...] = jnp.zeros_like(zero_ref)
        n_used = (starts_ref[ne - 1] + pc_ref[ne - 1]) // te

        def start(e, carry):
            @pl.when(pc_ref[e] > 0)
            def _():
                zero_tile(last_tile(e)).start()
            return carry

        def wait(e, carry):
            @pl.when(pc_ref[e] > 0)
            def _():
                zero_tile(last_tile(e)).wait()
            return carry

        def start_tail(tile, carry):
            zero_tile(tile).start()
            return carry

        def wait_tail(tile, carry):
            zero_tile(tile).wait()
            return carry

        lax.fori_loop(0, ne, start, 0)
        lax.fori_loop(n_used, n_tiles, start_tail, 0)
        lax.fori_loop(0, ne, wait, 0)
        lax.fori_loop(n_used, n_tiles, wait_tail, 0)

    stage = stage_ref.at[slot]
    _store_rows_as_tiles(stage, x_ref[...], tm)

    @pl.when(step > 0)
    def _():
        wait_rows()

    def row(r, carry):
        for kk in range(TOP_K):
            p = pos_ref[0, 0, kk * tm + r]
            pltpu.make_async_copy(_tile_of_row(stage, r), _tile_of_row(xs_ref, p),
                                  sem).start(priority=kk % 2)
        return carry

    lax.fori_loop(0, tm, row, 0, unroll=ROW_DMA_UNROLL)

    @pl.when(step == pl.num_programs(0) - 1)
    def _():
        wait_rows()


def _dispatch(x1, pos_tiles, starts, pc, n_slots, tm, te):
    t, d = x1.shape
    assert d == ROW_SUB * ROW_LANE
    grid_spec = pltpu.PrefetchScalarGridSpec(
        num_scalar_prefetch=2,
        grid=(t // tm,),
        in_specs=[pl.BlockSpec((1, 1, tm * TOP_K), lambda i, *_: (i, 0, 0),
                               memory_space=pltpu.SMEM),
                  pl.BlockSpec((tm, d), lambda i, *_: (i, 0))],
        out_specs=pl.BlockSpec(memory_space=pl.ANY),
        scratch_shapes=[pltpu.VMEM((te * ROW_SUB, ROW_LANE), F32),
                        pltpu.VMEM((2, tm * ROW_SUB, ROW_LANE), F32),
                        pltpu.SemaphoreType.DMA,
                        pltpu.SemaphoreType.DMA],
    )
    return pl.pallas_call(
        functools.partial(_dispatch_kernel, tm=tm, te=te),
        grid_spec=grid_spec,
        out_shape=jax.ShapeDtypeStruct((n_slots * ROW_SUB, ROW_LANE), F32),
        compiler_params=_params("arbitrary"),
        name="moe_dispatch",
    )(starts, pc, pos_tiles, x1)


DEINT_CHUNK = 512


def _deinterleave_kernel(w_ref, p_ref, g_ref, l_ref):
    half = DEINT_CHUNK // 2
    r = _dot(w_ref[0].astype(BF16), p_ref[...])
    g_ref[0] = r[:, :half].astype(BF16)
    l_ref[0] = r[:, half:].astype(BF16)


def _deinterleave(w1):
    ne, d, two_dff = w1.shape
    c = DEINT_CHUNK
    half = c // 2
    j = jnp.arange(c)
    perm = (j[None, :] == jnp.where(j % 2 == 0, j // 2, half + j // 2)[:, None]).astype(BF16)
    out = jax.ShapeDtypeStruct((ne, d, two_dff // 2), BF16)
    return pl.pallas_call(
        _deinterleave_kernel,
        grid=(ne, two_dff // c),
        in_specs=[pl.BlockSpec((1, d, c), lambda e, k: (e, 0, k)),
                  pl.BlockSpec((c, c), lambda e, k: (0, 0))],
        out_specs=[pl.BlockSpec((1, d, half), lambda e, k: (e, 0, k)),
                   pl.BlockSpec((1, d, half), lambda e, k: (e, 0, k))],
        out_shape=[out, out],
        compiler_params=_params("arbitrary", "arbitrary"),
        name="w1_deinterleave",
    )(w1, perm)


def _expert_kernel(te_ref, nu_ref, x_ref, w1g_ref, w1l_ref, b1g_ref, b1l_ref, w2_ref, b2_ref,
                   o_ref, *, te):
    @pl.when(pl.program_id(0) < nu_ref[0])
    def _():
        xb = _load_tiles_as_rows(x_ref, te).astype(BF16)
        hg = _dot(xb, w1g_ref[0]) + b1g_ref[0]
        hl = _dot(xb, w1l_ref[0]) + b1l_ref[0]
        glu = jnp.minimum(hg, SWIGLU_LIMIT)
        lin = jnp.clip(hl, -SWIGLU_LIMIT, SWIGLU_LIMIT)
        act = glu * _sigmoid(SWIGLU_ALPHA * glu) * (lin + 1.0)
        _store_rows_as_tiles(o_ref, _dot(act.astype(BF16), w2_ref[0]) + b2_ref[0], te)

    @pl.when(pl.program_id(0) >= nu_ref[0])
    def _():
        o_ref[...] = jnp.zeros_like(o_ref)


def _experts(xs, tile_expert, n_used, w1g, w1l, b1g, b1l, w2, b2, te):
    d = w1g.shape[1]
    dff = w1g.shape[2]
    n_tiles = xs.shape[0] // (te * ROW_SUB)
    blk = (te * ROW_SUB, ROW_LANE)

    def row(i, te_ref, nu_ref):
        return (jnp.minimum(i, nu_ref[0] - 1), 0)

    def wsel(i, te_ref, nu_ref):
        return (te_ref[i], 0, 0)

    grid_spec = pltpu.PrefetchScalarGridSpec(
        num_scalar_prefetch=2,
        grid=(n_tiles,),
        in_specs=[pl.BlockSpec(blk, row),
                  pl.BlockSpec((1, d, dff), wsel), pl.BlockSpec((1, d, dff), wsel),
                  pl.BlockSpec((1, 1, dff), wsel), pl.BlockSpec((1, 1, dff), wsel),
                  pl.BlockSpec((1, dff, d), wsel), pl.BlockSpec((1, 1, d), wsel)],
        out_specs=pl.BlockSpec(blk, lambda i, te_ref, nu_ref: (i, 0)),
    )
    return pl.pallas_call(
        functools.partial(_expert_kernel, te=te),
        grid_spec=grid_spec,
        out_shape=jax.ShapeDtypeStruct(xs.shape, F32),
        compiler_params=_params("arbitrary"),
        name="moe_experts",
    )(tile_expert, n_used, xs, w1g, w1l, b1g, b1l, w2, b2)


def _combine_kernel(pos_ref, pos_next_ref, x_ref, gate_ref, lnw_ref, lnb_ref, eo_ref, o_ref,
                    buf_ref, sem, *, tm):
    step = pl.program_id(0)
    slot = step % 2

    def gather(p_ref, slot_):
        def row(r, carry):
            for kk in range(TOP_K):
                p = p_ref[0, 0, kk * tm + r]
                pltpu.make_async_copy(_tile_of_row(eo_ref, p),
                                      _tile_of_row(buf_ref.at[slot_, kk], r),
                                      sem.at[slot_]).start(priority=kk % 2)
            return carry

        lax.fori_loop(0, tm, row, 0, unroll=ROW_DMA_UNROLL)

    @pl.when(step == 0)
    def _():
        gather(pos_ref, slot)

    @pl.when(step + 1 < pl.num_programs(0))
    def _():
        gather(pos_next_ref, 1 - slot)

    for kk in range(TOP_K):
        pltpu.make_async_copy(eo_ref.at[pl.ds(0, tm * ROW_SUB)], buf_ref.at[slot, kk],
                              sem.at[slot]).wait()

    g = gate_ref[...]
    y = g[:, 0:1] * _load_tiles_as_rows(buf_ref.at[slot, 0], tm)
    for kk in range(1, TOP_K):
        y = y + g[:, kk:kk + 1] * _load_tiles_as_rows(buf_ref.at[slot, kk], tm)
    o_ref[...] = _layer_norm(DEEPNORM_ALPHA * x_ref[...] + y, lnw_ref[...], lnb_ref[...])


def _combine(eo, pos_tiles, x1, gates, ln_w, ln_b, tm):
    t, d = x1.shape
    nt = t // tm
    row = lambda i: (i, 0)
    fixed = lambda i: (0, 0)
    pos_spec = lambda f: pl.BlockSpec((1, 1, tm * TOP_K), f, memory_space=pltpu.SMEM)
    return pl.pallas_call(
        functools.partial(_combine_kernel, tm=tm),
        grid=(nt,),
        in_specs=[pos_spec(lambda i: (i, 0, 0)),
                  pos_spec(lambda i: (jnp.minimum(i + 1, nt - 1), 0, 0)),
                  pl.BlockSpec((tm, d), row),
                  pl.BlockSpec((tm, TOP_K), row),
                  pl.BlockSpec((1, d), fixed), pl.BlockSpec((1, d), fixed),
                  pl.BlockSpec(memory_space=pl.ANY)],
        out_specs=pl.BlockSpec((tm, d), row),
        out_shape=jax.ShapeDtypeStruct((t, d), F32),
        scratch_shapes=[pltpu.VMEM((2, TOP_K, tm * ROW_SUB, ROW_LANE), F32),
                        pltpu.SemaphoreType.DMA((2,))],
        compiler_params=_params("arbitrary"),
        name="moe_combine",
    )(pos_tiles, pos_tiles, x1, gates, ln_w, ln_b, eo)


def _moe(x1, idx, gates, rank, counts, w1, b1, w2, b2, ln_w, ln_b, tm=256, te=256):
    t, d = x1.shape
    tm = min(tm, t)
    ne = w1.shape[0]
    counts = counts[:, 0]
    pc = ((counts + te - 1) // te) * te
    ends = jnp.cumsum(pc)
    starts = ends - pc
    n_tiles = (t * TOP_K) // te + ne
    n_used = (ends[-1] // te).astype(I32)
    tile_ids = jnp.arange(n_tiles, dtype=I32)
    te_raw = jnp.sum((tile_ids[:, None] >= (ends // te)[None, :]).astype(I32), axis=1)
    te_raw = jnp.minimum(te_raw, ne - 1)
    tile_expert = jnp.where(tile_ids < n_used, te_raw, jnp.take(te_raw, n_used - 1))
    e_ids = jnp.arange(ne, dtype=I32)[:, None, None]
    start_of = jnp.sum(jnp.where(idx[None] == e_ids, starts.astype(I32)[:, None, None], 0), axis=0)
    pos = start_of + rank
    pos_tiles = pos.reshape(TOP_K, t // tm, tm).transpose(1, 0, 2).reshape(t // tm, 1, TOP_K * tm)
    gates = gates.T

    w1g, w1l = _deinterleave(w1)
    b1g = b1[:, None, 0::2]
    b1l = b1[:, None, 1::2]

    xs = _dispatch(x1, pos_tiles, starts.astype(I32), pc.astype(I32), n_tiles * te, tm, te)
    eo = _experts(xs, tile_expert, n_used.reshape(1), w1g, w1l, b1g, b1l,
                  w2.astype(BF16), b2[:, None, :], te)
    return _combine(eo, pos_tiles, x1, gates, ln_w, ln_b, tm)


ATTN_GROUP = 4
ATTN_UNROLL = 2


def _attn_kernel(q_ref, kt_ref, v_ref, nu_ref, o_ref, qs_ref, z0_ref, z1_ref, sp0_ref, sp1_ref,
                 w0_ref, w1_ref, acc_ref, carry_ref, *, tq):
    z_refs, sp_refs, w_refs = (z0_ref, z1_ref), (sp0_ref, sp1_ref), (w0_ref, w1_ref)
    i = pl.program_id(2)
    hd = C_HEAD_DIM
    g = ATTN_GROUP
    width = g * hd
    q = q_ref[...]
    lane = lax.broadcasted_iota(I32, (tq, width), 1)
    for h in range(g):
        in_head = (lane >= h * hd) & (lane < (h + 1) * hd)
        qs_ref[h * tq:(h + 1) * tq, :] = jnp.where(in_head, q, jnp.zeros_like(q))
    neg_u = nu_ref[...]
    row = lax.broadcasted_iota(I32, (g * tq, tq), 0) & (tq - 1)
    strict = lax.broadcasted_iota(I32, (g * tq, tq), 1) < row
    vmask = [(lane >= h * hd) & (lane < (h + 1) * hd) for h in range(g)]

    def key_rows(n):
        return pl.ds(pl.multiple_of(jnp.maximum(i - n, 0) * tq, tq), tq)

    def scores(n, slot, diag):
        z = _dot(qs_ref[...], kt_ref[jnp.maximum(i - n, 0)])
        z_refs[slot][...] = z
        zb = z.astype(BF16)
        sp = jnp.maximum(zb, 0) + jnp.log(1 + jnp.exp(-jnp.abs(zb)))
        if diag:
            sp = jnp.where(strict, sp, jnp.zeros_like(sp))
        sp_refs[slot][...] = sp

    def weights(slot, diag):
        suf = _dot(sp_refs[slot][...], neg_u)
        w = jnp.exp(z_refs[slot][...] + suf + carry_ref[...])
        if diag:
            w = jnp.where(strict, w, 0.0)
        w = w.astype(BF16)
        for h in range(g):
            w_refs[slot][:, h * tq:(h + 1) * tq] = w[h * tq:(h + 1) * tq, :]
        carry_ref[...] = carry_ref[...] + suf[:, 0:1]

    def output(n, slot, valid):
        vj = v_ref[key_rows(n), :]
        v_cat = jnp.concatenate([jnp.where(m, vj, jnp.zeros_like(vj)) for m in vmask], axis=0)
        pv = _dot(w_refs[slot][...], v_cat)
        if valid is not None:
            pv = jnp.where(valid, pv, 0.0)
        acc_ref[...] = acc_ref[...] + pv

    carry_ref[...] = jnp.zeros_like(carry_ref)
    acc_ref[...] = jnp.zeros_like(acc_ref)
    scores(0, 0, True)
    weights(0, True)
    scores(1, 1, False)

    def body(p, carry):
        for u in range(ATTN_UNROLL):
            t = 2 + ATTN_UNROLL * p + u
            output(t - 2, u % 2, None if u == 0 else (t - 2 <= i))
            weights((u + 1) % 2, False)
            scores(t, u % 2, False)
        return carry

    lax.fori_loop(0, (i + ATTN_UNROLL) // ATTN_UNROLL, body, 0)
    o_ref[...] = acc_ref[...].astype(o_ref.dtype)


def _attention(qkv, bsz, seq, width, tq=256):
    tq = min(tq, seq)
    blk = ATTN_GROUP * C_HEAD_DIM
    ngroups = width // blk
    nq = seq // tq
    neg_u = -(jnp.arange(tq)[:, None] >= jnp.arange(tq)[None, :]).astype(BF16)
    qkv3 = qkv.reshape(bsz, seq, 3 * width)
    kt = qkv3[:, :, width:2 * width].reshape(bsz, nq, tq, width).transpose(0, 1, 3, 2)
    return pl.pallas_call(
        functools.partial(_attn_kernel, tq=tq),
        grid=(bsz, ngroups, nq),
        in_specs=[pl.BlockSpec((None, tq, blk), lambda b, p, i: (b, i, p)),
                  pl.BlockSpec((None, nq, blk, tq), lambda b, p, i: (b, 0, p, 0)),
                  pl.BlockSpec((None, seq, blk), lambda b, p, i: (b, 0, 2 * ngroups + p)),
                  pl.BlockSpec((tq, tq), lambda b, p, i: (0, 0))],
        out_specs=pl.BlockSpec((None, tq, blk), lambda b, p, i: (b, i, p)),
        out_shape=jax.ShapeDtypeStruct((bsz, seq, width), BF16),
        scratch_shapes=[pltpu.VMEM((ATTN_GROUP * tq, blk), BF16),
                        pltpu.VMEM((ATTN_GROUP * tq, tq), F32),
                        pltpu.VMEM((ATTN_GROUP * tq, tq), F32),
                        pltpu.VMEM((ATTN_GROUP * tq, tq), BF16),
                        pltpu.VMEM((ATTN_GROUP * tq, tq), BF16),
                        pltpu.VMEM((tq, ATTN_GROUP * tq), BF16),
                        pltpu.VMEM((tq, ATTN_GROUP * tq), BF16),
                        pltpu.VMEM((tq, blk), F32),
                        pltpu.VMEM((ATTN_GROUP * tq, 1), F32)],
        compiler_params=_params("arbitrary", "arbitrary", "arbitrary"),
        name="stickbreak_attn",
    )(qkv3, kt, qkv3, neg_u).reshape(bsz * seq, width)


def kernel(x, hgrn_lb_logits, l0_w_in, l0_a_norm_w, l0_w_pool, l0_pool_scale, l0_w_o, l0_ln1_w, l0_ln1_b, l0_w_router, l0_b_router, l0_w1, l0_b1, l0_w2, l0_b2, l0_ln2_w, l0_ln2_b, l1_w_in, l1_w_o, l1_ln1_w, l1_ln1_b, l1_w_router, l1_b_router, l1_w1, l1_b1, l1_w2, l1_b2, l1_ln2_w, l1_ln2_b):
    bsz, seq, d = x.shape
    out_dtype = x.dtype
    x2d = x.reshape(bsz * seq, d).astype(F32)
    r1 = lambda a: a.reshape(1, -1).astype(F32)

    a_width = A_HEADS * A_DK
    lb = jnp.cumsum(jax.nn.softmax(hgrn_lb_logits.astype(F32), axis=0), axis=0)[0]
    proj0 = _proj(x2d, l0_w_in.astype(BF16), F32)
    a_out = _hgrn(proj0, r1(lb), r1(l0_a_norm_w), bsz, seq)
    b_width = l0_w_pool.shape[0] * l0_w_pool.shape[1]
    b_out = _pool(proj0, l0_w_pool.astype(BF16), r1(l0_pool_scale), bsz, seq,
                  col_block=(4 * a_width) // b_width)
    x1, idx, gates, rank, counts = _mix_router(
        [a_out, b_out], l0_w_o.astype(BF16), x2d, r1(l0_ln1_w), r1(l0_ln1_b),
        l0_w_router.astype(F32), r1(l0_b_router))
    x2d = _moe(x1, idx, gates, rank, counts, l0_w1, l0_b1, l0_w2, l0_b2,
               r1(l0_ln2_w), r1(l0_ln2_b))

    c_width = l1_w_in.shape[1] // 3
    qscale = jnp.concatenate([jnp.full((c_width,), C_HEAD_DIM ** -0.5, F32),
                              jnp.ones((2 * c_width,), F32)])
    qkv = _proj(x2d, (l1_w_in * qscale).astype(BF16), BF16)
    o = _attention(qkv, bsz, seq, c_width)
    x1, idx, gates, rank, counts = _mix_router(
        [o], l1_w_o.astype(BF16), x2d, r1(l1_ln1_w), r1(l1_ln1_b),
        l1_w_router.astype(F32), r1(l1_b_router))
    x2d = _moe(x1, idx, gates, rank, counts, l1_w1, l1_b1, l1_w2, l1_b2,
               r1(l1_ln2_w), r1(l1_ln2_b))
    return x2d.reshape(bsz, seq, d).astype(out_dtype)
```

```python
import functools

import jax
import jax.numpy as jnp
from jax import lax
from jax.experimental import pallas as pl
from jax.experimental.pallas import tpu as pltpu

F32 = jnp.float32
BF16 = jnp.bfloat16
I32 = jnp.int32

DEPTH = 2
DEEPNORM_ALPHA = (2.0 * DEPTH) ** 0.25
LN_EPS = 1e-5
RMS_EPS = 1e-6

A_HEADS = 4
A_DK = 128
POOL_WINDOWS = (2, 4, 8, 16)
POOL_HALO = 16
C_HEAD_DIM = 64
N_EXPERTS = 32
TOP_K = 4
SWIGLU_ALPHA = 1.702
SWIGLU_LIMIT = 7.0

HGRN_CHUNK = 16
ROW_DMA_UNROLL = 8
VMEM_LIMIT = 48 * 1024 * 1024


def _params(*sem):
    return pltpu.CompilerParams(dimension_semantics=sem, vmem_limit_bytes=VMEM_LIMIT)


def _dot(a, b):
    return jnp.dot(a, b, preferred_element_type=F32)


def _dot_nt(a, b):
    return lax.dot_general(a, b, (((1,), (1,)), ((), ())), preferred_element_type=F32)


def _dot_tn(a, b):
    return lax.dot_general(a, b, (((0,), (0,)), ((), ())), preferred_element_type=F32)


def _sigmoid(x):
    e = jnp.exp(-jnp.abs(x))
    r = 1.0 / (1.0 + e)
    return jnp.where(x >= 0, r, e * r)


def _layer_norm(y, w, b):
    mu = jnp.mean(y, axis=-1, keepdims=True)
    yc = y - mu
    var = jnp.mean(yc * yc, axis=-1, keepdims=True)
    return yc * lax.rsqrt(var + LN_EPS) * w + b


def _proj_kernel(x_ref, w_ref, o_ref):
    o_ref[...] = _dot(x_ref[...].astype(BF16), w_ref[...]).astype(o_ref.dtype)


def _proj(x2d, w, out_dtype, tm=512):
    m, k = x2d.shape
    n = w.shape[1]
    tm = min(tm, m)
    return pl.pallas_call(
        _proj_kernel,
        grid=(m // tm,),
        in_specs=[pl.BlockSpec((tm, k), lambda i: (i, 0)),
                  pl.BlockSpec((k, n), lambda i: (0, 0))],
        out_specs=pl.BlockSpec((tm, n), lambda i: (i, 0)),
        out_shape=jax.ShapeDtypeStruct((m, n), out_dtype),
        compiler_params=_params("arbitrary"),
        name="proj",
    )(x2d, w)


def _hgrn_kernel(q_ref, f_ref, i_ref, g_ref, lb_ref, nw_ref, ltri_ref, o_ref,
                 st_ref, cum_ref, kk_ref, oacc_ref, *, ts):
    c = HGRN_CHUNK

    @pl.when(pl.program_id(1) == 0)
    def _():
        st_ref[...] = jnp.zeros_like(st_ref)

    z = f_ref[...]
    lb = lb_ref[...]
    e = jnp.exp(-jnp.abs(z))
    r = 1.0 / (1.0 + e)
    er = e * r
    sig = jnp.where(z >= 0, r, er)
    sig_neg = jnp.where(z >= 0, er, r)
    logf = jnp.log(lb + (1.0 - lb) * sig)
    kk_ref[...] = (1.0 - lb) * sig_neg

    ltri = ltri_ref[...]
    x1 = logf.astype(BF16)
    r1 = logf - x1.astype(F32)
    x2 = r1.astype(BF16)
    x3 = (r1 - x2.astype(F32)).astype(BF16)
    cum_ref[...] = _dot(ltri, x1) + _dot(ltri, x2) + _dot(ltri, x3)

    t_iota = lax.broadcasted_iota(I32, (c, A_DK), 0)

    def chunk(ci, carry):
        r0 = pl.multiple_of(ci * c, c)
        for h in range(A_HEADS):
            cols = slice(h * A_DK, (h + 1) * A_DK)
            cum = cum_ref[pl.ds(r0, c), cols]
            q = q_ref[pl.ds(r0, c), cols]
            k = kk_ref[pl.ds(r0, c), cols]
            v = i_ref[pl.ds(r0, c), cols]
            last = cum[c - 1:c, :]
            st = st_ref[h]
            qd = (q * jnp.exp(cum)).astype(BF16)
            o = _dot_nt(qd, st.astype(BF16))
            for s in range(c):
                d = jnp.where(t_iota >= s, cum - cum[s:s + 1, :], -1e30)
                p = q * (k[s:s + 1, :] * jnp.exp(d))
                o = o + jnp.sum(p, axis=-1, keepdims=True) * v[s:s + 1, :]
            oacc_ref[pl.ds(r0, c), cols] = o
            kd = (k * jnp.exp(last - cum)).astype(BF16)
            st_ref[h] = jnp.exp(last) * st + _dot_tn(v.astype(BF16), kd)
        return carry

    lax.fori_loop(0, ts // c, chunk, 0)

    g = g_ref[...]
    gate = g * _sigmoid(g)
    for h in range(A_HEADS):
        cols = slice(h * A_DK, (h + 1) * A_DK)
        o = oacc_ref[:, cols]
        o = o * lax.rsqrt(jnp.mean(o * o, axis=-1, keepdims=True) + RMS_EPS) * nw_ref[...]
        o_ref[:, cols] = o * gate[:, cols]


def _hgrn(proj, lb, norm_w, bsz, seq, ts=256):
    ts = min(ts, seq)
    ns = seq // ts
    h = A_HEADS
    blk = jnp.arange(ts) // HGRN_CHUNK
    ltri = ((blk[:, None] == blk[None, :]) &
            (jnp.arange(ts)[:, None] >= jnp.arange(ts)[None, :])).astype(BF16)

    wd = h * A_DK

    def col(part):
        return pl.BlockSpec((ts, wd), lambda b, s, part=part: (b * ns + s, part))

    return pl.pallas_call(
        functools.partial(_hgrn_kernel, ts=ts),
        grid=(bsz, ns),
        in_specs=[col(0), col(1), col(2), col(3),
                  pl.BlockSpec((1, wd), lambda b, s: (0, 0)),
                  pl.BlockSpec((1, A_DK), lambda b, s: (0, 0)),
                  pl.BlockSpec((ts, ts), lambda b, s: (0, 0))],
        out_specs=pl.BlockSpec((ts, wd), lambda b, s: (b * ns + s, 0)),
        out_shape=jax.ShapeDtypeStruct((bsz * seq, wd), F32),
        scratch_shapes=[pltpu.VMEM((h, A_DK, A_DK), F32),
                        pltpu.VMEM((ts, wd), F32),
                        pltpu.VMEM((ts, wd), F32),
                        pltpu.VMEM((ts, wd), F32)],
        compiler_params=_params("arbitrary", "arbitrary"),
        name="hgrn2",
    )(proj, proj, proj, proj, lb, norm_w, ltri)


def _pool_kernel(v_ref, wp_ref, sc_ref, o_ref, buf_ref, *, ts):
    s = pl.program_id(1)
    hl = POOL_HALO

    @pl.when(s == 0)
    def _():
        buf_ref[0:hl, :] = jnp.zeros((hl, buf_ref.shape[1]), F32)

    x = v_ref[...]
    buf_ref[hl:hl + ts, :] = x
    pos = (s * ts + 1 + lax.broadcasted_iota(I32, (ts, 1), 0)).astype(F32)
    for g, win in enumerate(POOL_WINDOWS):
        lo, hi = g * 128, (g + 1) * 128
        xg = x[:, lo:hi]
        acc = xg
        for j in range(1, win):
            acc = acc + buf_ref[hl - j:hl - j + ts, lo:hi]
        pooled = acc / jnp.minimum(pos, float(win)) - xg
        o_ref[:, lo:hi] = _dot(pooled.astype(BF16), wp_ref[g]) * sc_ref[:, lo:hi]
    buf_ref[0:hl, :] = x[ts - hl:ts, :]


def _pool(proj, w_pool, pool_scale, bsz, seq, col_block, ts=512):
    ts = min(ts, seq)
    ns = seq // ts
    width = w_pool.shape[0] * w_pool.shape[1]
    return pl.pallas_call(
        functools.partial(_pool_kernel, ts=ts),
        grid=(bsz, ns),
        in_specs=[pl.BlockSpec((ts, width), lambda b, s: (b * ns + s, col_block)),
                  pl.BlockSpec(w_pool.shape, lambda b, s: (0, 0, 0)),
                  pl.BlockSpec((1, width), lambda b, s: (0, 0))],
        out_specs=pl.BlockSpec((ts, width), lambda b, s: (b * ns + s, 0)),
        out_shape=jax.ShapeDtypeStruct((bsz * seq, width), F32),
        scratch_shapes=[pltpu.VMEM((POOL_HALO + ts, width), F32)],
        compiler_params=_params("arbitrary", "arbitrary"),
        name="pool",
    )(proj, w_pool, pool_scale)


def _mix_router_kernel(*refs, n_parts, tm):
    parts = refs[:n_parts]
    wos = refs[n_parts:2 * n_parts]
    x_ref, lnw_ref, lnb_ref, wr_ref, br_ref, tri_ref = refs[2 * n_parts:2 * n_parts + 6]
    x1_ref, idx_ref, gate_ref, rank_ref, cnt_ref, run_ref = refs[2 * n_parts + 6:]

    @pl.when(pl.program_id(0) == 0)
    def _():
        run_ref[...] = jnp.zeros_like(run_ref)

    h = _dot(parts[0][...].astype(BF16), wos[0][...])
    for p_ref, w_ref in zip(parts[1:], wos[1:]):
        h = h + _dot(p_ref[...].astype(BF16), w_ref[...])
    x1 = _layer_norm(DEEPNORM_ALPHA * x_ref[...] + h, lnw_ref[...], lnb_ref[...])
    x1_ref[...] = x1

    ne = br_ref.shape[0]
    x_hi = x1.astype(BF16)
    x_mid = (x1 - x_hi.astype(F32)).astype(BF16)
    hi_mid = _dot_nt(wr_ref[...], x_hi)
    logits = hi_mid[:ne] + hi_mid[ne:] + _dot_nt(wr_ref[:ne, :], x_mid) + br_ref[...]
    e_iota = lax.broadcasted_iota(I32, (ne, tm), 0).astype(F32)
    k_iota = lax.broadcasted_iota(I32, (TOP_K, tm), 0)
    l = logits
    vals, idxs = [], []
    for _ in range(TOP_K):
        m = jnp.max(l, axis=0, keepdims=True)
        ik = jnp.min(jnp.where(l == m, e_iota, float(ne)), axis=0, keepdims=True)
        vals.append(m)
        idxs.append(ik)
        l = jnp.where(e_iota == ik, -jnp.inf, l)
    exps = [jnp.exp(v - vals[0]) for v in vals]
    tot = exps[0] + exps[1] + exps[2] + exps[3]

    onehot = jnp.zeros((ne, tm), F32)
    for ik in idxs:
        onehot = onehot + (e_iota == ik).astype(F32)
    base = _dot(onehot.astype(BF16), tri_ref[...]) + run_ref[...]

    idx_out = jnp.zeros((TOP_K, tm), I32)
    gate_out = jnp.zeros((TOP_K, tm), F32)
    rank_out = jnp.zeros((TOP_K, tm), I32)
    for kk in range(TOP_K):
        rk = jnp.sum(jnp.where(e_iota == idxs[kk], base, 0.0), axis=0, keepdims=True)
        idx_out = jnp.where(k_iota == kk, idxs[kk].astype(I32), idx_out)
        gate_out = jnp.where(k_iota == kk, exps[kk] / tot, gate_out)
        rank_out = jnp.where(k_iota == kk, rk.astype(I32), rank_out)
    idx_ref[...] = idx_out
    gate_ref[...] = gate_out
    rank_ref[...] = rank_out
    run_ref[...] = run_ref[...] + jnp.sum(onehot, axis=1, keepdims=True)
    cnt_ref[...] = run_ref[...].astype(I32)


def _mix_router(parts, w_o, x2d, ln_w, ln_b, w_router, b_router, tm=256):
    t, d = x2d.shape
    tm = min(tm, t)
    n_parts = len(parts)
    wos, off = [], 0
    for p in parts:
        wos.append(lax.slice_in_dim(w_o, off, off + p.shape[1], axis=0))
        off += p.shape[1]
    tri = (jnp.arange(tm)[:, None] < jnp.arange(tm)[None, :]).astype(BF16)
    ne = w_router.shape[1]
    row = lambda i: (i, 0)
    col = lambda i: (0, i)
    fixed = lambda i: (0, 0)
    in_specs = ([pl.BlockSpec((tm, p.shape[1]), row) for p in parts] +
                [pl.BlockSpec(w.shape, fixed) for w in wos] +
                [pl.BlockSpec((tm, d), row),
                 pl.BlockSpec((1, d), fixed), pl.BlockSpec((1, d), fixed),
                 pl.BlockSpec((2 * ne, d), fixed), pl.BlockSpec((ne, 1), fixed),
                 pl.BlockSpec((tm, tm), fixed)])
    wr_t = w_router.T
    wr_hi = wr_t.astype(BF16)
    wr_mid = (wr_t - wr_hi.astype(F32)).astype(BF16)
    wr_split = jnp.concatenate([wr_hi, wr_mid], axis=0)
    out_specs = [pl.BlockSpec((tm, d), row),
                 pl.BlockSpec((TOP_K, tm), col),
                 pl.BlockSpec((TOP_K, tm), col),
                 pl.BlockSpec((TOP_K, tm), col),
                 pl.BlockSpec((ne, 1), fixed)]
    out_shape = [jax.ShapeDtypeStruct((t, d), F32),
                 jax.ShapeDtypeStruct((TOP_K, t), I32),
                 jax.ShapeDtypeStruct((TOP_K, t), F32),
                 jax.ShapeDtypeStruct((TOP_K, t), I32),
                 jax.ShapeDtypeStruct((ne, 1), I32)]
    return pl.pallas_call(
        functools.partial(_mix_router_kernel, n_parts=n_parts, tm=tm),
        grid=(t // tm,),
        in_specs=in_specs,
        out_specs=out_specs,
        out_shape=out_shape,
        scratch_shapes=[pltpu.VMEM((ne, 1), F32)],
        compiler_params=_params("arbitrary"),
        name="mix_router",
    )(*parts, *wos, x2d, ln_w, ln_b, wr_split, b_router.reshape(ne, 1), tri)


ROW_SUB = 8
ROW_LANE = 128


def _store_rows_as_tiles(ref, x, n):
    for s in range(ROW_SUB):
        ref[pl.ds(s, n, stride=ROW_SUB), :] = x[:, s * ROW_LANE:(s + 1) * ROW_LANE]


def _load_tiles_as_rows(ref, n):
    return jnp.concatenate([ref[pl.ds(s, n, stride=ROW_SUB), :] for s in range(ROW_SUB)], axis=1)


def _tile_of_row(ref, r):
    return ref.at[pl.ds(pl.multiple_of(r * ROW_SUB, ROW_SUB), ROW_SUB)]


def _dispatch_kernel(starts_ref, pc_ref, pos_ref, x_ref, xs_ref, zero_ref, stage_ref, sem, zsem,
                     *, tm, te):
    ne = starts_ref.shape[0]
    n_tiles = xs_ref.shape[0] // (te * ROW_SUB)
    step = pl.program_id(0)
    slot = step % 2

    def zero_tile(tile):
        rows = te * ROW_SUB
        return pltpu.make_async_copy(
            zero_ref, xs_ref.at[pl.ds(pl.multiple_of(tile * rows, rows), rows)], zsem)

    def wait_rows():
        for kk in range(TOP_K):
            pltpu.make_async_copy(stage_ref.at[0], xs_ref.at[pl.ds(0, tm * ROW_SUB)], sem).wait()

    def last_tile(e):
        return (starts_ref[e] + pc_ref[e]) // te - 1

    @pl.when(pl.program_id(0) == 0)
    def _():
        zero_ref[...] = jnp.zeros_like(zero_ref)
        n_used = (starts_ref[ne - 1] + pc_ref[ne - 1]) // te

        def start(e, carry):
            @pl.when(pc_ref[e] > 0)
            def _():
                zero_tile(last_tile(e)).start()
            return carry

        def wait(e, carry):
            @pl.when(pc_ref[e] > 0)
            def _():
                zero_tile(last_tile(e)).wait()
            return carry

        def start_tail(tile, carry):
            zero_tile(tile).start()
            return carry

        def wait_tail(tile, carry):
            zero_tile(tile).wait()
            return carry

        lax.fori_loop(0, ne, start, 0)
        lax.fori_loop(n_used, n_tiles, start_tail, 0)
        lax.fori_loop(0, ne, wait, 0)
        lax.fori_loop(n_used, n_tiles, wait_tail, 0)

    stage = stage_ref.at[slot]
    _store_rows_as_tiles(stage, x_ref[...], tm)

    @pl.when(step > 0)
    def _():
        wait_rows()

    def row(r, carry):
        for kk in range(TOP_K):
            p = pos_ref[0, 0, kk * tm + r]
            pltpu.make_async_copy(_tile_of_row(stage, r), _tile_of_row(xs_ref, p),
                                  sem).start(priority=kk % 2)
        return carry

    lax.fori_loop(0, tm, row, 0, unroll=ROW_DMA_UNROLL)

    @pl.when(step == pl.num_programs(0) - 1)
    def _():
        wait_rows()


def _dispatch(x1, pos_tiles, starts, pc, n_slots, tm, te):
    t, d = x1.shape
    assert d == ROW_SUB * ROW_LANE
    grid_spec = pltpu.PrefetchScalarGridSpec(
        num_scalar_prefetch=2,
        grid=(t // tm,),
        in_specs=[pl.BlockSpec((1, 1, tm * TOP_K), lambda i, *_: (i, 0, 0),
                               memory_space=pltpu.SMEM),
                  pl.BlockSpec((tm, d), lambda i, *_: (i, 0))],
        out_specs=pl.BlockSpec(memory_space=pl.ANY),
        scratch_shapes=[pltpu.VMEM((te * ROW_SUB, ROW_LANE), F32),
                        pltpu.VMEM((2, tm * ROW_SUB, ROW_LANE), F32),
                        pltpu.SemaphoreType.DMA,
                        pltpu.SemaphoreType.DMA],
    )
    return pl.pallas_call(
        functools.partial(_dispatch_kernel, tm=tm, te=te),
        grid_spec=grid_spec,
        out_shape=jax.ShapeDtypeStruct((n_slots * ROW_SUB, ROW_LANE), F32),
        compiler_params=_params("arbitrary"),
        name="moe_dispatch",
    )(starts, pc, pos_tiles, x1)


DEINT_CHUNK = 512


def _deinterleave_kernel(w_ref, p_ref, g_ref, l_ref):
    half = DEINT_CHUNK // 2
    r = _dot(w_ref[0].astype(BF16), p_ref[...])
    g_ref[0] = r[:, :half].astype(BF16)
    l_ref[0] = r[:, half:].astype(BF16)


def _deinterleave(w1):
    ne, d, two_dff = w1.shape
    c = DEINT_CHUNK
    half = c // 2
    j = jnp.arange(c)
    perm = (j[None, :] == jnp.where(j % 2 == 0, j // 2, half + j // 2)[:, None]).astype(BF16)
    out = jax.ShapeDtypeStruct((ne, d, two_dff // 2), BF16)
    return pl.pallas_call(
        _deinterleave_kernel,
        grid=(ne, two_dff // c),
        in_specs=[pl.BlockSpec((1, d, c), lambda e, k: (e, 0, k)),
                  pl.BlockSpec((c, c), lambda e, k: (0, 0))],
        out_specs=[pl.BlockSpec((1, d, half), lambda e, k: (e, 0, k)),
                   pl.BlockSpec((1, d, half), lambda e, k: (e, 0, k))],
        out_shape=[out, out],
        compiler_params=_params("arbitrary", "arbitrary"),
        name="w1_deinterleave",
    )(w1, perm)


def _expert_kernel(te_ref, nu_ref, x_ref, w1g_ref, w1l_ref, b1g_ref, b1l_ref, w2_ref, b2_ref,
                   o_ref, *, te):
    @pl.when(pl.program_id(0) < nu_ref[0])
    def _():
        xb = _load_tiles_as_rows(x_ref, te).astype(BF16)
        hg = _dot(xb, w1g_ref[0]) + b1g_ref[0]
        hl = _dot(xb, w1l_ref[0]) + b1l_ref[0]
        glu = jnp.minimum(hg, SWIGLU_LIMIT)
        lin = jnp.clip(hl, -SWIGLU_LIMIT, SWIGLU_LIMIT)
        act = glu * _sigmoid(SWIGLU_ALPHA * glu) * (lin + 1.0)
        _store_rows_as_tiles(o_ref, _dot(act.astype(BF16), w2_ref[0]) + b2_ref[0], te)

    @pl.when(pl.program_id(0) >= nu_ref[0])
    def _():
        o_ref[...] = jnp.zeros_like(o_ref)


def _experts(xs, tile_expert, n_used, w1g, w1l, b1g, b1l, w2, b2, te):
    d = w1g.shape[1]
    dff = w1g.shape[2]
    n_tiles = xs.shape[0] // (te * ROW_SUB)
    blk = (te * ROW_SUB, ROW_LANE)

    def row(i, te_ref, nu_ref):
        return (jnp.minimum(i, nu_ref[0] - 1), 0)

    def wsel(i, te_ref, nu_ref):
        return (te_ref[i], 0, 0)

    grid_spec = pltpu.PrefetchScalarGridSpec(
        num_scalar_prefetch=2,
        grid=(n_tiles,),
        in_specs=[pl.BlockSpec(blk, row),
                  pl.BlockSpec((1, d, dff), wsel), pl.BlockSpec((1, d, dff), wsel),
                  pl.BlockSpec((1, 1, dff), wsel), pl.BlockSpec((1, 1, dff), wsel),
                  pl.BlockSpec((1, dff, d), wsel), pl.BlockSpec((1, 1, d), wsel)],
        out_specs=pl.BlockSpec(blk, lambda i, te_ref, nu_ref: (i, 0)),
    )
    return pl.pallas_call(
        functools.partial(_expert_kernel, te=te),
        grid_spec=grid_spec,
        out_shape=jax.ShapeDtypeStruct(xs.shape, F32),
        compiler_params=_params("arbitrary"),
        name="moe_experts",
    )(tile_expert, n_used, xs, w1g, w1l, b1g, b1l, w2, b2)


def _combine_kernel(pos_ref, pos_next_ref, x_ref, gate_ref, lnw_ref, lnb_ref, eo_ref, o_ref,
                    buf_ref, sem, *, tm):
    step = pl.program_id(0)
    slot = step % 2

    def gather(p_ref, slot_):
        def row(r, carry):
            for kk in range(TOP_K):
                p = p_ref[0, 0, kk * tm + r]
                pltpu.make_async_copy(_tile_of_row(eo_ref, p),
                                      _tile_of_row(buf_ref.at[slot_, kk], r),
                                      sem.at[slot_]).start(priority=kk % 2)
            return carry

        lax.fori_loop(0, tm, row, 0, unroll=ROW_DMA_UNROLL)

    @pl.when(step == 0)
    def _():
        gather(pos_ref, slot)

    @pl.when(step + 1 < pl.num_programs(0))
    def _():
        gather(pos_next_ref, 1 - slot)

    for kk in range(TOP_K):
        pltpu.make_async_copy(eo_ref.at[pl.ds(0, tm * ROW_SUB)], buf_ref.at[slot, kk],
                              sem.at[slot]).wait()

    g = gate_ref[...]
    y = g[:, 0:1] * _load_tiles_as_rows(buf_ref.at[slot, 0], tm)
    for kk in range(1, TOP_K):
        y = y + g[:, kk:kk + 1] * _load_tiles_as_rows(buf_ref.at[slot, kk], tm)
    o_ref[...] = _layer_norm(DEEPNORM_ALPHA * x_ref[...] + y, lnw_ref[...], lnb_ref[...])


def _combine(eo, pos_tiles, x1, gates, ln_w, ln_b, tm):
    t, d = x1.shape
    nt = t // tm
    row = lambda i: (i, 0)
    fixed = lambda i: (0, 0)
    pos_spec = lambda f: pl.BlockSpec((1, 1, tm * TOP_K), f, memory_space=pltpu.SMEM)
    return pl.pallas_call(
        functools.partial(_combine_kernel, tm=tm),
        grid=(nt,),
        in_specs=[pos_spec(lambda i: (i, 0, 0)),
                  pos_spec(lambda i: (jnp.minimum(i + 1, nt - 1), 0, 0)),
                  pl.BlockSpec((tm, d), row),
                  pl.BlockSpec((tm, TOP_K), row),
                  pl.BlockSpec((1, d), fixed), pl.BlockSpec((1, d), fixed),
                  pl.BlockSpec(memory_space=pl.ANY)],
        out_specs=pl.BlockSpec((tm, d), row),
        out_shape=jax.ShapeDtypeStruct((t, d), F32),
        scratch_shapes=[pltpu.VMEM((2, TOP_K, tm * ROW_SUB, ROW_LANE), F32),
                        pltpu.SemaphoreType.DMA((2,))],
        compiler_params=_params("arbitrary"),
        name="moe_combine",
    )(pos_tiles, pos_tiles, x1, gates, ln_w, ln_b, eo)


def _moe(x1, idx, gates, rank, counts, w1, b1, w2, b2, ln_w, ln_b, tm=256, te=256):
    t, d = x1.shape
    tm = min(tm, t)
    ne = w1.shape[0]
    counts = counts[:, 0]
    pc = ((counts + te - 1) // te) * te
    ends = jnp.cumsum(pc)
    starts = ends - pc
    n_tiles = (t * TOP_K) // te + ne
    n_used = (ends[-1] // te).astype(I32)
    tile_ids = jnp.arange(n_tiles, dtype=I32)
    te_raw = jnp.sum((tile_ids[:, None] >= (ends // te)[None, :]).astype(I32), axis=1)
    te_raw = jnp.minimum(te_raw, ne - 1)
    tile_expert = jnp.where(tile_ids < n_used, te_raw, jnp.take(te_raw, n_used - 1))
    e_ids = jnp.arange(ne, dtype=I32)[:, None, None]
    start_of = jnp.sum(jnp.where(idx[None] == e_ids, starts.astype(I32)[:, None, None], 0), axis=0)
    pos = start_of + rank
    pos_tiles = pos.reshape(TOP_K, t // tm, tm).transpose(1, 0, 2).reshape(t // tm, 1, TOP_K * tm)
    gates = gates.T

    w1g, w1l = _deinterleave(w1)
    b1g = b1[:, None, 0::2]
    b1l = b1[:, None, 1::2]

    xs = _dispatch(x1, pos_tiles, starts.astype(I32), pc.astype(I32), n_tiles * te, tm, te)
    eo = _experts(xs, tile_expert, n_used.reshape(1), w1g, w1l, b1g, b1l,
                  w2.astype(BF16), b2[:, None, :], te)
    return _combine(eo, pos_tiles, x1, gates, ln_w, ln_b, tm)


ATTN_GROUP = 4
ATTN_UNROLL = 2
ATTN_LOG_CUTOFF = -110.0


def _attn_kernel(q_ref, kt_ref, v_ref, nu_ref, o_ref, qs_ref, z0_ref, z1_ref, sp0_ref, sp1_ref,
                 w0_ref, w1_ref, acc_ref, carry_ref, *, tq):
    z_refs, sp_refs, w_refs = (z0_ref, z1_ref), (sp0_ref, sp1_ref), (w0_ref, w1_ref)
    i = pl.program_id(2)
    hd = C_HEAD_DIM
    g = ATTN_GROUP
    width = g * hd
    q = q_ref[...]
    lane = lax.broadcasted_iota(I32, (tq, width), 1)
    for h in range(g):
        in_head = (lane >= h * hd) & (lane < (h + 1) * hd)
        qs_ref[h * tq:(h + 1) * tq, :] = jnp.where(in_head, q, jnp.zeros_like(q))
    neg_u = nu_ref[...]
    row = lax.broadcasted_iota(I32, (g * tq, tq), 0) & (tq - 1)
    strict = lax.broadcasted_iota(I32, (g * tq, tq), 1) < row
    vmask = [(lane >= h * hd) & (lane < (h + 1) * hd) for h in range(g)]

    def key_rows(n):
        return pl.ds(pl.multiple_of(jnp.maximum(i - n, 0) * tq, tq), tq)

    def scores(n, slot, diag):
        z = _dot(qs_ref[...], kt_ref[jnp.maximum(i - n, 0)])
        z_refs[slot][...] = z
        zb = z.astype(BF16)
        sp = jnp.maximum(zb, 0) + jnp.log(1 + jnp.exp(-jnp.abs(zb)))
        if diag:
            sp = jnp.where(strict, sp, jnp.zeros_like(sp))
        sp_refs[slot][...] = sp

    def weights(slot, diag):
        suf = _dot(sp_refs[slot][...], neg_u)
        w = jnp.exp(z_refs[slot][...] + suf + carry_ref[...])
        if diag:
            w = jnp.where(strict, w, 0.0)
        w = w.astype(BF16)
        for h in range(g):
            w_refs[slot][:, h * tq:(h + 1) * tq] = w[h * tq:(h + 1) * tq, :]
        carry_ref[...] = carry_ref[...] + suf[:, 0:1]

    def output(n, slot, valid):
        vj = v_ref[key_rows(n), :]
        v_cat = jnp.concatenate([jnp.where(m, vj, jnp.zeros_like(vj)) for m in vmask], axis=0)
        pv = _dot(w_refs[slot][...], v_cat)
        if valid is not None:
            pv = jnp.where(valid, pv, 0.0)
        acc_ref[...] = acc_ref[...] + pv

    carry_ref[...] = jnp.zeros_like(carry_ref)
    acc_ref[...] = jnp.zeros_like(acc_ref)
    scores(0, 0, True)
    weights(0, True)
    scores(1, 1, False)

    def body(st):
        p, _ = st
        for u in range(ATTN_UNROLL):
            t = 2 + ATTN_UNROLL * p + u
            output(t - 2, u % 2, None if u == 0 else (t - 2 <= i))
            weights((u + 1) % 2, False)
            scores(t, u % 2, False)
        return p + 1, jnp.max(carry_ref[...])

    def keep_going(st):
        p, carry_max = st
        return (p < (i + ATTN_UNROLL) // ATTN_UNROLL) & (carry_max >= ATTN_LOG_CUTOFF)

    groups_done, _ = lax.while_loop(keep_going, body, (jnp.int32(0), jnp.max(carry_ref[...])))
    pending = ATTN_UNROLL * groups_done
    output(pending, 0, pending <= i)
    o_ref[...] = acc_ref[...].astype(o_ref.dtype)


def _attention(qkv, bsz, seq, width, tq=256):
    tq = min(tq, seq)
    blk = ATTN_GROUP * C_HEAD_DIM
    ngroups = width // blk
    nq = seq // tq
    neg_u = -(jnp.arange(tq)[:, None] >= jnp.arange(tq)[None, :]).astype(BF16)
    qkv3 = qkv.reshape(bsz, seq, 3 * width)
    kt = qkv3[:, :, width:2 * width].reshape(bsz, nq, tq, width).transpose(0, 1, 3, 2)
    return pl.pallas_call(
        functools.partial(_attn_kernel, tq=tq),
        grid=(bsz, ngroups, nq),
        in_specs=[pl.BlockSpec((None, tq, blk), lambda b, p, i: (b, i, p)),
                  pl.BlockSpec((None, nq, blk, tq), lambda b, p, i: (b, 0, p, 0)),
                  pl.BlockSpec((None, seq, blk), lambda b, p, i: (b, 0, 2 * ngroups + p)),
                  pl.BlockSpec((tq, tq), lambda b, p, i: (0, 0))],
        out_specs=pl.BlockSpec((None, tq, blk), lambda b, p, i: (b, i, p)),
        out_shape=jax.ShapeDtypeStruct((bsz, seq, width), BF16),
        scratch_shapes=[pltpu.VMEM((ATTN_GROUP * tq, blk), BF16),
                        pltpu.VMEM((ATTN_GROUP * tq, tq), F32),
                        pltpu.VMEM((ATTN_GROUP * tq, tq), F32),
                        pltpu.VMEM((ATTN_GROUP * tq, tq), BF16),
                        pltpu.VMEM((ATTN_GROUP * tq, tq), BF16),
                        pltpu.VMEM((tq, ATTN_GROUP * tq), BF16),
                        pltpu.VMEM((tq, ATTN_GROUP * tq), BF16),
                        pltpu.VMEM((tq, blk), F32),
                        pltpu.VMEM((ATTN_GROUP * tq, 1), F32)],
        compiler_params=_params("arbitrary", "arbitrary", "arbitrary"),
        name="stickbreak_attn",
    )(qkv3, kt, qkv3, neg_u).reshape(bsz * seq, width)


def kernel(x, hgrn_lb_logits, l0_w_in, l0_a_norm_w, l0_w_pool, l0_pool_scale, l0_w_o, l0_ln1_w, l0_ln1_b, l0_w_router, l0_b_router, l0_w1, l0_b1, l0_w2, l0_b2, l0_ln2_w, l0_ln2_b, l1_w_in, l1_w_o, l1_ln1_w, l1_ln1_b, l1_w_router, l1_b_router, l1_w1, l1_b1, l1_w2, l1_b2, l1_ln2_w, l1_ln2_b):
    bsz, seq, d = x.shape
    out_dtype = x.dtype
    x2d = x.reshape(bsz * seq, d).astype(F32)
    r1 = lambda a: a.reshape(1, -1).astype(F32)

    a_width = A_HEADS * A_DK
    lb = jnp.cumsum(jax.nn.softmax(hgrn_lb_logits.astype(F32), axis=0), axis=0)[0]
    proj0 = _proj(x2d, l0_w_in.astype(BF16), F32)
    a_out = _hgrn(proj0, r1(lb), r1(l0_a_norm_w), bsz, seq)
    b_width = l0_w_pool.shape[0] * l0_w_pool.shape[1]
    b_out = _pool(proj0, l0_w_pool.astype(BF16), r1(l0_pool_scale), bsz, seq,
                  col_block=(4 * a_width) // b_width)
    x1, idx, gates, rank, counts = _mix_router(
        [a_out, b_out], l0_w_o.astype(BF16), x2d, r1(l0_ln1_w), r1(l0_ln1_b),
        l0_w_router.astype(F32), r1(l0_b_router))
    x2d = _moe(x1, idx, gates, rank, counts, l0_w1, l0_b1, l0_w2, l0_b2,
               r1(l0_ln2_w), r1(l0_ln2_b))

    c_width = l1_w_in.shape[1] // 3
    qscale = jnp.concatenate([jnp.full((c_width,), C_HEAD_DIM ** -0.5, F32),
                              jnp.ones((2 * c_width,), F32)])
    qkv = _proj(x2d, (l1_w_in * qscale).astype(BF16), BF16)
    o = _attention(qkv, bsz, seq, c_width)
    x1, idx, gates, rank, counts = _mix_router(
        [o], l1_w_o.astype(BF16), x2d, r1(l1_ln1_w), r1(l1_ln1_b),
        l1_w_router.astype(F32), r1(l1_b_router))
    x2d = _moe(x1, idx, gates, rank, counts, l1_w1, l1_b1, l1_w2, l1_b2,
               r1(l1_ln2_w), r1(l1_ln2_b))
    return x2d.reshape(bsz, seq, d).astype(out_dtype)
```

```python
import functools

import jax
import jax.numpy as jnp
from jax import lax
from jax.experimental import pallas as pl
from jax.experimental.pallas import tpu as pltpu

F32 = jnp.float32
BF16 = jnp.bfloat16
I32 = jnp.int32

DEPTH = 2
DEEPNORM_ALPHA = (2.0 * DEPTH) ** 0.25
LN_EPS = 1e-5
RMS_EPS = 1e-6

A_HEADS = 4
A_DK = 128
POOL_WINDOWS = (2, 4, 8, 16)
POOL_HALO = 16
C_HEAD_DIM = 64
N_EXPERTS = 32
TOP_K = 4
SWIGLU_ALPHA = 1.702
SWIGLU_LIMIT = 7.0

HGRN_CHUNK = 16
ROW_DMA_UNROLL = 8
VMEM_LIMIT = 48 * 1024 * 1024


def _params(*sem):
    return pltpu.CompilerParams(dimension_semantics=sem, vmem_limit_bytes=VMEM_LIMIT)


def _dot(a, b):
    return jnp.dot(a, b, preferred_element_type=F32)


def _dot_nt(a, b):
    return lax.dot_general(a, b, (((1,), (1,)), ((), ())), preferred_element_type=F32)


def _dot_tn(a, b):
    return lax.dot_general(a, b, (((0,), (0,)), ((), ())), preferred_element_type=F32)


def _sigmoid(x):
    e = jnp.exp(-jnp.abs(x))
    r = 1.0 / (1.0 + e)
    return jnp.where(x >= 0, r, e * r)


def _layer_norm(y, w, b):
    mu = jnp.mean(y, axis=-1, keepdims=True)
    yc = y - mu
    var = jnp.mean(yc * yc, axis=-1, keepdims=True)
    return yc * lax.rsqrt(var + LN_EPS) * w + b


def _proj_kernel(x_ref, w_ref, o_ref):
    o_ref[...] = _dot(x_ref[...].astype(BF16), w_ref[...]).astype(o_ref.dtype)


def _proj(x2d, w, out_dtype, tm=512):
    m, k = x2d.shape
    n = w.shape[1]
    tm = min(tm, m)
    return pl.pallas_call(
        _proj_kernel,
        grid=(m // tm,),
        in_specs=[pl.BlockSpec((tm, k), lambda i: (i, 0)),
                  pl.BlockSpec((k, n), lambda i: (0, 0))],
        out_specs=pl.BlockSpec((tm, n), lambda i: (i, 0)),
        out_shape=jax.ShapeDtypeStruct((m, n), out_dtype),
        compiler_params=_params("arbitrary"),
        name="proj",
    )(x2d, w)


def _hgrn_kernel(q_ref, f_ref, i_ref, g_ref, lb_ref, nw_ref, ltri_ref, o_ref,
                 st_ref, cum_ref, kk_ref, oacc_ref, *, ts):
    c = HGRN_CHUNK

    @pl.when(pl.program_id(1) == 0)
    def _():
        st_ref[...] = jnp.zeros_like(st_ref)

    z = f_ref[...]
    lb = lb_ref[...]
    e = jnp.exp(-jnp.abs(z))
    r = 1.0 / (1.0 + e)
    er = e * r
    sig = jnp.where(z >= 0, r, er)
    sig_neg = jnp.where(z >= 0, er, r)
    logf = jnp.log(lb + (1.0 - lb) * sig)
    kk_ref[...] = (1.0 - lb) * sig_neg

    ltri = ltri_ref[...]
    x1 = logf.astype(BF16)
    r1 = logf - x1.astype(F32)
    x2 = r1.astype(BF16)
    x3 = (r1 - x2.astype(F32)).astype(BF16)
    cum_ref[...] = _dot(ltri, x1) + _dot(ltri, x2) + _dot(ltri, x3)

    t_iota = lax.broadcasted_iota(I32, (c, A_DK), 0)

    def chunk(ci, carry):
        r0 = pl.multiple_of(ci * c, c)
        for h in range(A_HEADS):
            cols = slice(h * A_DK, (h + 1) * A_DK)
            cum = cum_ref[pl.ds(r0, c), cols]
            q = q_ref[pl.ds(r0, c), cols]
            k = kk_ref[pl.ds(r0, c), cols]
            v = i_ref[pl.ds(r0, c), cols]
            last = cum[c - 1:c, :]
            st = st_ref[h]
            qd = (q * jnp.exp(cum)).astype(BF16)
            o = _dot_nt(qd, st.astype(BF16))
            for s in range(c):
                d = jnp.where(t_iota >= s, cum - cum[s:s + 1, :], -1e30)
                p = q * (k[s:s + 1, :] * jnp.exp(d))
                o = o + jnp.sum(p, axis=-1, keepdims=True) * v[s:s + 1, :]
            oacc_ref[pl.ds(r0, c), cols] = o
            kd = (k * jnp.exp(last - cum)).astype(BF16)
            st_ref[h] = jnp.exp(last) * st + _dot_tn(v.astype(BF16), kd)
        return carry

    lax.fori_loop(0, ts // c, chunk, 0)

    g = g_ref[...]
    gate = g * _sigmoid(g)
    for h in range(A_HEADS):
        cols = slice(h * A_DK, (h + 1) * A_DK)
        o = oacc_ref[:, cols]
        o = o * lax.rsqrt(jnp.mean(o * o, axis=-1, keepdims=True) + RMS_EPS) * nw_ref[...]
        o_ref[:, cols] = o * gate[:, cols]


def _hgrn(proj, lb, norm_w, bsz, seq, ts=256):
    ts = min(ts, seq)
    ns = seq // ts
    h = A_HEADS
    blk = jnp.arange(ts) // HGRN_CHUNK
    ltri = ((blk[:, None] == blk[None, :]) &
            (jnp.arange(ts)[:, None] >= jnp.arange(ts)[None, :])).astype(BF16)

    wd = h * A_DK

    def col(part):
        return pl.BlockSpec((ts, wd), lambda b, s, part=part: (b * ns + s, part))

    return pl.pallas_call(
        functools.partial(_hgrn_kernel, ts=ts),
        grid=(bsz, ns),
        in_specs=[col(0), col(1), col(2), col(3),
                  pl.BlockSpec((1, wd), lambda b, s: (0, 0)),
                  pl.BlockSpec((1, A_DK), lambda b, s: (0, 0)),
                  pl.BlockSpec((ts, ts), lambda b, s: (0, 0))],
        out_specs=pl.BlockSpec((ts, wd), lambda b, s: (b * ns + s, 0)),
        out_shape=jax.ShapeDtypeStruct((bsz * seq, wd), F32),
        scratch_shapes=[pltpu.VMEM((h, A_DK, A_DK), F32),
                        pltpu.VMEM((ts, wd), F32),
                        pltpu.VMEM((ts, wd), F32),
                        pltpu.VMEM((ts, wd), F32)],
        compiler_params=_params("arbitrary", "arbitrary"),
        name="hgrn2",
    )(proj, proj, proj, proj, lb, norm_w, ltri)


def _pool_kernel(v_ref, wp_ref, sc_ref, o_ref, buf_ref, *, ts):
    s = pl.program_id(1)
    hl = POOL_HALO

    @pl.when(s == 0)
    def _():
        buf_ref[0:hl, :] = jnp.zeros((hl, buf_ref.shape[1]), F32)

    x = v_ref[...]
    buf_ref[hl:hl + ts, :] = x
    pos = (s * ts + 1 + lax.broadcasted_iota(I32, (ts, 1), 0)).astype(F32)
    for g, win in enumerate(POOL_WINDOWS):
        lo, hi = g * 128, (g + 1) * 128
        xg = x[:, lo:hi]
        acc = xg
        for j in range(1, win):
            acc = acc + buf_ref[hl - j:hl - j + ts, lo:hi]
        pooled = acc / jnp.minimum(pos, float(win)) - xg
        o_ref[:, lo:hi] = _dot(pooled.astype(BF16), wp_ref[g]) * sc_ref[:, lo:hi]
    buf_ref[0:hl, :] = x[ts - hl:ts, :]


def _pool(proj, w_pool, pool_scale, bsz, seq, col_block, ts=512):
    ts = min(ts, seq)
    ns = seq // ts
    width = w_pool.shape[0] * w_pool.shape[1]
    return pl.pallas_call(
        functools.partial(_pool_kernel, ts=ts),
        grid=(bsz, ns),
        in_specs=[pl.BlockSpec((ts, width), lambda b, s: (b * ns + s, col_block)),
                  pl.BlockSpec(w_pool.shape, lambda b, s: (0, 0, 0)),
                  pl.BlockSpec((1, width), lambda b, s: (0, 0))],
        out_specs=pl.BlockSpec((ts, width), lambda b, s: (b * ns + s, 0)),
        out_shape=jax.ShapeDtypeStruct((bsz * seq, width), F32),
        scratch_shapes=[pltpu.VMEM((POOL_HALO + ts, width), F32)],
        compiler_params=_params("arbitrary", "arbitrary"),
        name="pool",
    )(proj, w_pool, pool_scale)


def _mix_router_kernel(*refs, n_parts, tm):
    parts = refs[:n_parts]
    wos = refs[n_parts:2 * n_parts]
    x_ref, lnw_ref, lnb_ref, wr_ref, br_ref, tri_ref = refs[2 * n_parts:2 * n_parts + 6]
    x1_ref, idx_ref, gate_ref, rank_ref, cnt_ref, run_ref = refs[2 * n_parts + 6:]

    @pl.when(pl.program_id(0) == 0)
    def _():
        run_ref[...] = jnp.zeros_like(run_ref)

    h = _dot(parts[0][...].astype(BF16), wos[0][...])
    for p_ref, w_ref in zip(parts[1:], wos[1:]):
        h = h + _dot(p_ref[...].astype(BF16), w_ref[...])
    x1 = _layer_norm(DEEPNORM_ALPHA * x_ref[...] + h, lnw_ref[...], lnb_ref[...])
    x1_ref[...] = x1

    ne = br_ref.shape[0]
    x_hi = x1.astype(BF16)
    x_mid = (x1 - x_hi.astype(F32)).astype(BF16)
    hi_mid = _dot_nt(wr_ref[...], x_hi)
    logits = hi_mid[:ne] + hi_mid[ne:] + _dot_nt(wr_ref[:ne, :], x_mid) + br_ref[...]
    e_iota = lax.broadcasted_iota(I32, (ne, tm), 0).astype(F32)
    k_iota = lax.broadcasted_iota(I32, (TOP_K, tm), 0)
    l = logits
    vals, idxs = [], []
    for _ in range(TOP_K):
        m = jnp.max(l, axis=0, keepdims=True)
        ik = jnp.min(jnp.where(l == m, e_iota, float(ne)), axis=0, keepdims=True)
        vals.append(m)
        idxs.append(ik)
        l = jnp.where(e_iota == ik, -jnp.inf, l)
    exps = [jnp.exp(v - vals[0]) for v in vals]
    tot = exps[0] + exps[1] + exps[2] + exps[3]

    onehot = jnp.zeros((ne, tm), F32)
    for ik in idxs:
        onehot = onehot + (e_iota == ik).astype(F32)
    base = _dot(onehot.astype(BF16), tri_ref[...]) + run_ref[...]

    idx_out = jnp.zeros((TOP_K, tm), I32)
    gate_out = jnp.zeros((TOP_K, tm), F32)
    rank_out = jnp.zeros((TOP_K, tm), I32)
    for kk in range(TOP_K):
        rk = jnp.sum(jnp.where(e_iota == idxs[kk], base, 0.0), axis=0, keepdims=True)
        idx_out = jnp.where(k_iota == kk, idxs[kk].astype(I32), idx_out)
        gate_out = jnp.where(k_iota == kk, exps[kk] / tot, gate_out)
        rank_out = jnp.where(k_iota == kk, rk.astype(I32), rank_out)
    idx_ref[...] = idx_out
    gate_ref[...] = gate_out
    rank_ref[...] = rank_out
    run_ref[...] = run_ref[...] + jnp.sum(onehot, axis=1, keepdims=True)
    cnt_ref[...] = run_ref[...].astype(I32)


def _mix_router(parts, w_o, x2d, ln_w, ln_b, w_router, b_router, tm=256):
    t, d = x2d.shape
    tm = min(tm, t)
    n_parts = len(parts)
    wos, off = [], 0
    for p in parts:
        wos.append(lax.slice_in_dim(w_o, off, off + p.shape[1], axis=0))
        off += p.shape[1]
    tri = (jnp.arange(tm)[:, None] < jnp.arange(tm)[None, :]).astype(BF16)
    ne = w_router.shape[1]
    row = lambda i: (i, 0)
    col = lambda i: (0, i)
    fixed = lambda i: (0, 0)
    in_specs = ([pl.BlockSpec((tm, p.shape[1]), row) for p in parts] +
                [pl.BlockSpec(w.shape, fixed) for w in wos] +
                [pl.BlockSpec((tm, d), row),
                 pl.BlockSpec((1, d), fixed), pl.BlockSpec((1, d), fixed),
                 pl.BlockSpec((2 * ne, d), fixed), pl.BlockSpec((ne, 1), fixed),
                 pl.BlockSpec((tm, tm), fixed)])
    wr_t = w_router.T
    wr_hi = wr_t.astype(BF16)
    wr_mid = (wr_t - wr_hi.astype(F32)).astype(BF16)
    wr_split = jnp.concatenate([wr_hi, wr_mid], axis=0)
    out_specs = [pl.BlockSpec((tm, d), row),
                 pl.BlockSpec((TOP_K, tm), col),
                 pl.BlockSpec((TOP_K, tm), col),
                 pl.BlockSpec((TOP_K, tm), col),
                 pl.BlockSpec((ne, 1), fixed)]
    out_shape = [jax.ShapeDtypeStruct((t, d), F32),
                 jax.ShapeDtypeStruct((TOP_K, t), I32),
                 jax.ShapeDtypeStruct((TOP_K, t), F32),
                 jax.ShapeDtypeStruct((TOP_K, t), I32),
                 jax.ShapeDtypeStruct((ne, 1), I32)]
    return pl.pallas_call(
        functools.partial(_mix_router_kernel, n_parts=n_parts, tm=tm),
        grid=(t // tm,),
        in_specs=in_specs,
        out_specs=out_specs,
        out_shape=out_shape,
        scratch_shapes=[pltpu.VMEM((ne, 1), F32)],
        compiler_params=_params("arbitrary"),
        name="mix_router",
    )(*parts, *wos, x2d, ln_w, ln_b, wr_split, b_router.reshape(ne, 1), tri)


ROW_SUB = 8
ROW_LANE = 128


def _store_rows_as_tiles(ref, x, n):
    for s in range(ROW_SUB):
        ref[pl.ds(s, n, stride=ROW_SUB), :] = x[:, s * ROW_LANE:(s + 1) * ROW_LANE]


def _load_tiles_as_rows(ref, n):
    return jnp.concatenate([ref[pl.ds(s, n, stride=ROW_SUB), :] for s in range(ROW_SUB)], axis=1)


def _tile_of_row(ref, r):
    return ref.at[pl.ds(pl.multiple_of(r * ROW_SUB, ROW_SUB), ROW_SUB)]


def _dispatch_kernel(starts_ref, pc_ref, pos_ref, x_ref, xs_ref, zero_ref, stage_ref, sem, zsem,
                     *, tm, te):
    ne = starts_ref.shape[0]
    n_tiles = xs_ref.shape[0] // (te * ROW_SUB)
    step = pl.program_id(0)
    slot = step % 2

    def zero_tile(tile):
        rows = te * ROW_SUB
        return pltpu.make_async_copy(
            zero_ref, xs_ref.at[pl.ds(pl.multiple_of(tile * rows, rows), rows)], zsem)

    def wait_rows():
        for kk in range(TOP_K):
            pltpu.make_async_copy(stage_ref.at[0], xs_ref.at[pl.ds(0, tm * ROW_SUB)], sem).wait()

    def last_tile(e):
        return (starts_ref[e] + pc_ref[e]) // te - 1

    @pl.when(pl.program_id(0) == 0)
    def _():
        zero_ref[...] = jnp.zeros_like(zero_ref)
        n_used = (starts_ref[ne - 1] + pc_ref[ne - 1]) // te

        def start(e, carry):
            @pl.when(pc_ref[e] > 0)
            def _():
                zero_tile(last_tile(e)).start()
            return carry

        def wait(e, carry):
            @pl.when(pc_ref[e] > 0)
            def _():
                zero_tile(last_tile(e)).wait()
            return carry

        def start_tail(tile, carry):
            zero_tile(tile).start()
            return carry

        def wait_tail(tile, carry):
            zero_tile(tile).wait()
            return carry

        lax.fori_loop(0, ne, start, 0)
        lax.fori_loop(n_used, n_tiles, start_tail, 0)
        lax.fori_loop(0, ne, wait, 0)
        lax.fori_loop(n_used, n_tiles, wait_tail, 0)

    stage = stage_ref.at[slot]
    _store_rows_as_tiles(stage, x_ref[...], tm)

    @pl.when(step > 0)
    def _():
        wait_rows()

    def row(r, carry):
        for kk in range(TOP_K):
            p = pos_ref[0, 0, kk * tm + r]
            pltpu.make_async_copy(_tile_of_row(stage, r), _tile_of_row(xs_ref, p),
                                  sem).start(priority=kk % 2)
        return carry

    lax.fori_loop(0, tm, row, 0, unroll=ROW_DMA_UNROLL)

    @pl.when(step == pl.num_programs(0) - 1)
    def _():
        wait_rows()


def _dispatch(x1, pos_tiles, starts, pc, n_slots, tm, te):
    t, d = x1.shape
    assert d == ROW_SUB * ROW_LANE
    grid_spec = pltpu.PrefetchScalarGridSpec(
        num_scalar_prefetch=2,
        grid=(t // tm,),
        in_specs=[pl.BlockSpec((1, 1, tm * TOP_K), lambda i, *_: (i, 0, 0),
                               memory_space=pltpu.SMEM),
                  pl.BlockSpec((tm, d), lambda i, *_: (i, 0))],
        out_specs=pl.BlockSpec(memory_space=pl.ANY),
        scratch_shapes=[pltpu.VMEM((te * ROW_SUB, ROW_LANE), F32),
                        pltpu.VMEM((2, tm * ROW_SUB, ROW_LANE), F32),
                        pltpu.SemaphoreType.DMA,
                        pltpu.SemaphoreType.DMA],
    )
    return pl.pallas_call(
        functools.partial(_dispatch_kernel, tm=tm, te=te),
        grid_spec=grid_spec,
        out_shape=jax.ShapeDtypeStruct((n_slots * ROW_SUB, ROW_LANE), F32),
        compiler_params=_params("arbitrary"),
        name="moe_dispatch",
    )(starts, pc, pos_tiles, x1)


DEINT_CHUNK = 512


def _deinterleave_kernel(w_ref, p_ref, g_ref, l_ref):
    half = DEINT_CHUNK // 2
    r = _dot(w_ref[0].astype(BF16), p_ref[...])
    g_ref[0] = r[:, :half].astype(BF16)
    l_ref[0] = r[:, half:].astype(BF16)


def _deinterleave(w1):
    ne, d, two_dff = w1.shape
    c = DEINT_CHUNK
    half = c // 2
    j = jnp.arange(c)
    perm = (j[None, :] == jnp.where(j % 2 == 0, j // 2, half + j // 2)[:, None]).astype(BF16)
    out = jax.ShapeDtypeStruct((ne, d, two_dff // 2), BF16)
    return pl.pallas_call(
        _deinterleave_kernel,
        grid=(ne, two_dff // c),
        in_specs=[pl.BlockSpec((1, d, c), lambda e, k: (e, 0, k)),
                  pl.BlockSpec((c, c), lambda e, k: (0, 0))],
        out_specs=[pl.BlockSpec((1, d, half), lambda e, k: (e, 0, k)),
                   pl.BlockSpec((1, d, half), lambda e, k: (e, 0, k))],
        out_shape=[out, out],
        compiler_params=_params("arbitrary", "arbitrary"),
        name="w1_deinterleave",
    )(w1, perm)


def _expert_kernel(te_ref, nu_ref, x_ref, w1g_ref, w1l_ref, b1g_ref, b1l_ref, w2_ref, b2_ref,
                   o_ref, *, te):
    @pl.when(pl.program_id(0) < nu_ref[0])
    def _():
        xb = _load_tiles_as_rows(x_ref, te).astype(BF16)
        hg = _dot(xb, w1g_ref[0]) + b1g_ref[0]
        hl = _dot(xb, w1l_ref[0]) + b1l_ref[0]
        glu = jnp.minimum(hg, SWIGLU_LIMIT)
        lin = jnp.clip(hl, -SWIGLU_LIMIT, SWIGLU_LIMIT)
        act = glu * _sigmoid(SWIGLU_ALPHA * glu) * (lin + 1.0)
        _store_rows_as_tiles(o_ref, _dot(act.astype(BF16), w2_ref[0]) + b2_ref[0], te)

    @pl.when(pl.program_id(0) >= nu_ref[0])
    def _():
        o_ref[...] = jnp.zeros_like(o_ref)


def _experts(xs, tile_expert, n_used, w1g, w1l, b1g, b1l, w2, b2, te):
    d = w1g.shape[1]
    dff = w1g.shape[2]
    n_tiles = xs.shape[0] // (te * ROW_SUB)
    blk = (te * ROW_SUB, ROW_LANE)

    def row(i, te_ref, nu_ref):
        return (jnp.minimum(i, nu_ref[0] - 1), 0)

    def wsel(i, te_ref, nu_ref):
        return (te_ref[i], 0, 0)

    grid_spec = pltpu.PrefetchScalarGridSpec(
        num_scalar_prefetch=2,
        grid=(n_tiles,),
        in_specs=[pl.BlockSpec(blk, row),
                  pl.BlockSpec((1, d, dff), wsel), pl.BlockSpec((1, d, dff), wsel),
                  pl.BlockSpec((1, 1, dff), wsel), pl.BlockSpec((1, 1, dff), wsel),
                  pl.BlockSpec((1, dff, d), wsel), pl.BlockSpec((1, 1, d), wsel)],
        out_specs=pl.BlockSpec(blk, lambda i, te_ref, nu_ref: (i, 0)),
    )
    return pl.pallas_call(
        functools.partial(_expert_kernel, te=te),
        grid_spec=grid_spec,
        out_shape=jax.ShapeDtypeStruct(xs.shape, F32),
        compiler_params=_params("arbitrary"),
        name="moe_experts",
    )(tile_expert, n_used, xs, w1g, w1l, b1g, b1l, w2, b2)


def _combine_kernel(pos_ref, pos_next_ref, x_ref, gate_ref, lnw_ref, lnb_ref, eo_ref, o_ref,
                    buf_ref, sem, *, tm):
    step = pl.program_id(0)
    slot = step % 2

    def gather(p_ref, slot_):
        def row(r, carry):
            for kk in range(TOP_K):
                p = p_ref[0, 0, kk * tm + r]
                pltpu.make_async_copy(_tile_of_row(eo_ref, p),
                                      _tile_of_row(buf_ref.at[slot_, kk], r),
                                      sem.at[slot_]).start(priority=kk % 2)
            return carry

        lax.fori_loop(0, tm, row, 0, unroll=ROW_DMA_UNROLL)

    @pl.when(step == 0)
    def _():
        gather(pos_ref, slot)

    @pl.when(step + 1 < pl.num_programs(0))
    def _():
        gather(pos_next_ref, 1 - slot)

    for kk in range(TOP_K):
        pltpu.make_async_copy(eo_ref.at[pl.ds(0, tm * ROW_SUB)], buf_ref.at[slot, kk],
                              sem.at[slot]).wait()

    g = gate_ref[...]
    y = g[:, 0:1] * _load_tiles_as_rows(buf_ref.at[slot, 0], tm)
    for kk in range(1, TOP_K):
        y = y + g[:, kk:kk + 1] * _load_tiles_as_rows(buf_ref.at[slot, kk], tm)
    o_ref[...] = _layer_norm(DEEPNORM_ALPHA * x_ref[...] + y, lnw_ref[...], lnb_ref[...])


def _combine(eo, pos_tiles, x1, gates, ln_w, ln_b, tm):
    t, d = x1.shape
    nt = t // tm
    row = lambda i: (i, 0)
    fixed = lambda i: (0, 0)
    pos_spec = lambda f: pl.BlockSpec((1, 1, tm * TOP_K), f, memory_space=pltpu.SMEM)
    return pl.pallas_call(
        functools.partial(_combine_kernel, tm=tm),
        grid=(nt,),
        in_specs=[pos_spec(lambda i: (i, 0, 0)),
                  pos_spec(lambda i: (jnp.minimum(i + 1, nt - 1), 0, 0)),
                  pl.BlockSpec((tm, d), row),
                  pl.BlockSpec((tm, TOP_K), row),
                  pl.BlockSpec((1, d), fixed), pl.BlockSpec((1, d), fixed),
                  pl.BlockSpec(memory_space=pl.ANY)],
        out_specs=pl.BlockSpec((tm, d), row),
        out_shape=jax.ShapeDtypeStruct((t, d), F32),
        scratch_shapes=[pltpu.VMEM((2, TOP_K, tm * ROW_SUB, ROW_LANE), F32),
                        pltpu.SemaphoreType.DMA((2,))],
        compiler_params=_params("arbitrary"),
        name="moe_combine",
    )(pos_tiles, pos_tiles, x1, gates, ln_w, ln_b, eo)


def _moe(x1, idx, gates, rank, counts, w1, b1, w2, b2, ln_w, ln_b, tm=256, te=256):
    t, d = x1.shape
    tm = min(tm, t)
    ne = w1.shape[0]
    counts = counts[:, 0]
    pc = ((counts + te - 1) // te) * te
    ends = jnp.cumsum(pc)
    starts = ends - pc
    n_tiles = (t * TOP_K) // te + ne
    n_used = (ends[-1] // te).astype(I32)
    tile_ids = jnp.arange(n_tiles, dtype=I32)
    te_raw = jnp.sum((tile_ids[:, None] >= (ends // te)[None, :]).astype(I32), axis=1)
    te_raw = jnp.minimum(te_raw, ne - 1)
    tile_expert = jnp.where(tile_ids < n_used, te_raw, jnp.take(te_raw, n_used - 1))
    e_ids = jnp.arange(ne, dtype=I32)[:, None, None]
    start_of = jnp.sum(jnp.where(idx[None] == e_ids, starts.astype(I32)[:, None, None], 0), axis=0)
    pos = start_of + rank
    pos_tiles = pos.reshape(TOP_K, t // tm, tm).transpose(1, 0, 2).reshape(t // tm, 1, TOP_K * tm)
    gates = gates.T

    w1g, w1l = _deinterleave(w1)
    b1g = b1[:, None, 0::2]
    b1l = b1[:, None, 1::2]

    xs = _dispatch(x1, pos_tiles, starts.astype(I32), pc.astype(I32), n_tiles * te, tm, te)
    eo = _experts(xs, tile_expert, n_used.reshape(1), w1g, w1l, b1g, b1l,
                  w2.astype(BF16), b2[:, None, :], te)
    return _combine(eo, pos_tiles, x1, gates, ln_w, ln_b, tm)


ATTN_GROUP = 4
ATTN_LOG_CUTOFF = -110.0


def _attn_kernel(q_ref, kt_ref, v_ref, nu_ref, o_ref, qs_ref, z0_ref, z1_ref, sp0_ref, sp1_ref,
                 w0_ref, w1_ref, acc_ref, carry_ref, *, tq):
    z_refs, sp_refs, w_refs = (z0_ref, z1_ref), (sp0_ref, sp1_ref), (w0_ref, w1_ref)
    i = pl.program_id(2)
    hd = C_HEAD_DIM
    g = ATTN_GROUP
    width = g * hd
    q = q_ref[...]
    lane = lax.broadcasted_iota(I32, (tq, width), 1)
    for h in range(g):
        in_head = (lane >= h * hd) & (lane < (h + 1) * hd)
        qs_ref[h * tq:(h + 1) * tq, :] = jnp.where(in_head, q, jnp.zeros_like(q))
    neg_u = nu_ref[...]
    row = lax.broadcasted_iota(I32, (g * tq, tq), 0) & (tq - 1)
    strict = lax.broadcasted_iota(I32, (g * tq, tq), 1) < row
    vmask = [(lane >= h * hd) & (lane < (h + 1) * hd) for h in range(g)]

    def key_rows(n):
        return pl.ds(pl.multiple_of(jnp.maximum(i - n, 0) * tq, tq), tq)

    def scores(n, slot, diag):
        z = _dot(qs_ref[...], kt_ref[jnp.maximum(i - n, 0)])
        z_refs[slot][...] = z
        zb = z.astype(BF16)
        sp = jnp.maximum(zb, 0) + jnp.log(1 + jnp.exp(-jnp.abs(zb)))
        if diag:
            sp = jnp.where(strict, sp, jnp.zeros_like(sp))
        sp_refs[slot][...] = sp

    def weights(slot, diag):
        suf = _dot(sp_refs[slot][...], neg_u)
        w = jnp.exp(z_refs[slot][...] + suf + carry_ref[...])
        if diag:
            w = jnp.where(strict, w, 0.0)
        w = w.astype(BF16)
        for h in range(g):
            w_refs[slot][:, h * tq:(h + 1) * tq] = w[h * tq:(h + 1) * tq, :]
        carry_ref[...] = carry_ref[...] + suf[:, 0:1]

    def output(n, slot, valid):
        vj = v_ref[key_rows(n), :]
        v_cat = jnp.concatenate([jnp.where(m, vj, jnp.zeros_like(vj)) for m in vmask], axis=0)
        pv = _dot(w_refs[slot][...], v_cat)
        if valid is not None:
            pv = jnp.where(valid, pv, 0.0)
        acc_ref[...] = acc_ref[...] + pv

    carry_ref[...] = jnp.zeros_like(carry_ref)
    acc_ref[...] = jnp.zeros_like(acc_ref)
    scores(0, 0, True)
    weights(0, True)
    scores(1, 1, False)
    output(0, 0, None)
    weights(1, False)

    n_groups = (i + 1) // 2
    carry_max = jnp.max(carry_ref[...])

    @pl.when((n_groups > 0) & (carry_max >= ATTN_LOG_CUTOFF))
    def _():
        scores(2, 0, False)

    def body(st):
        p, _ = st
        t = 3 + 2 * p
        output(t - 2, 1, None)
        weights(0, False)
        scores(t, 1, False)
        output(t - 1, 0, t - 1 <= i)
        weights(1, False)
        scores(t + 1, 0, False)
        return p + 1, jnp.max(carry_ref[...])

    def keep_going(st):
        p, carry_max = st
        return (p < n_groups) & (carry_max >= ATTN_LOG_CUTOFF)

    groups_done, _ = lax.while_loop(keep_going, body, (jnp.int32(0), carry_max))
    pending = 1 + 2 * groups_done
    output(pending, 1, pending <= i)
    o_ref[...] = acc_ref[...].astype(o_ref.dtype)


def _attention(qkv, bsz, seq, width, tq=256):
    tq = min(tq, seq)
    blk = ATTN_GROUP * C_HEAD_DIM
    ngroups = width // blk
    nq = seq // tq
    neg_u = -(jnp.arange(tq)[:, None] >= jnp.arange(tq)[None, :]).astype(BF16)
    qkv3 = qkv.reshape(bsz, seq, 3 * width)
    kt = qkv3[:, :, width:2 * width].reshape(bsz, nq, tq, width).transpose(0, 1, 3, 2)
    return pl.pallas_call(
        functools.partial(_attn_kernel, tq=tq),
        grid=(bsz, ngroups, nq),
        in_specs=[pl.BlockSpec((None, tq, blk), lambda b, p, i: (b, i, p)),
                  pl.BlockSpec((None, nq, blk, tq), lambda b, p, i: (b, 0, p, 0)),
                  pl.BlockSpec((None, seq, blk), lambda b, p, i: (b, 0, 2 * ngroups + p)),
                  pl.BlockSpec((tq, tq), lambda b, p, i: (0, 0))],
        out_specs=pl.BlockSpec((None, tq, blk), lambda b, p, i: (b, i, p)),
        out_shape=jax.ShapeDtypeStruct((bsz, seq, width), BF16),
        scratch_shapes=[pltpu.VMEM((ATTN_GROUP * tq, blk), BF16),
                        pltpu.VMEM((ATTN_GROUP * tq, tq), F32),
                        pltpu.VMEM((ATTN_GROUP * tq, tq), F32),
                        pltpu.VMEM((ATTN_GROUP * tq, tq), BF16),
                        pltpu.VMEM((ATTN_GROUP * tq, tq), BF16),
                        pltpu.VMEM((tq, ATTN_GROUP * tq), BF16),
                        pltpu.VMEM((tq, ATTN_GROUP * tq), BF16),
                        pltpu.VMEM((tq, blk), F32),
                        pltpu.VMEM((ATTN_GROUP * tq, 1), F32)],
        compiler_params=_params("arbitrary", "arbitrary", "arbitrary"),
        name="stickbreak_attn",
    )(qkv3, kt, qkv3, neg_u).reshape(bsz * seq, width)


def kernel(x, hgrn_lb_logits, l0_w_in, l0_a_norm_w, l0_w_pool, l0_pool_scale, l0_w_o, l0_ln1_w, l0_ln1_b, l0_w_router, l0_b_router, l0_w1, l0_b1, l0_w2, l0_b2, l0_ln2_w, l0_ln2_b, l1_w_in, l1_w_o, l1_ln1_w, l1_ln1_b, l1_w_router, l1_b_router, l1_w1, l1_b1, l1_w2, l1_b2, l1_ln2_w, l1_ln2_b):
    bsz, seq, d = x.shape
    out_dtype = x.dtype
    x2d = x.reshape(bsz * seq, d).astype(F32)
    r1 = lambda a: a.reshape(1, -1).astype(F32)

    a_width = A_HEADS * A_DK
    lb = jnp.cumsum(jax.nn.softmax(hgrn_lb_logits.astype(F32), axis=0), axis=0)[0]
    proj0 = _proj(x2d, l0_w_in.astype(BF16), F32)
    a_out = _hgrn(proj0, r1(lb), r1(l0_a_norm_w), bsz, seq)
    b_width = l0_w_pool.shape[0] * l0_w_pool.shape[1]
    b_out = _pool(proj0, l0_w_pool.astype(BF16), r1(l0_pool_scale), bsz, seq,
                  col_block=(4 * a_width) // b_width)
    x1, idx, gates, rank, counts = _mix_router(
        [a_out, b_out], l0_w_o.astype(BF16), x2d, r1(l0_ln1_w), r1(l0_ln1_b),
        l0_w_router.astype(F32), r1(l0_b_router))
    x2d = _moe(x1, idx, gates, rank, counts, l0_w1, l0_b1, l0_w2, l0_b2,
               r1(l0_ln2_w), r1(l0_ln2_b))

    c_width = l1_w_in.shape[1] // 3
    qscale = jnp.concatenate([jnp.full((c_width,), C_HEAD_DIM ** -0.5, F32),
                              jnp.ones((2 * c_width,), F32)])
    qkv = _proj(x2d, (l1_w_in * qscale).astype(BF16), BF16)
    o = _attention(qkv, bsz, seq, c_width)
    x1, idx, gates, rank, counts = _mix_router(
        [o], l1_w_o.astype(BF16), x2d, r1(l1_ln1_w), r1(l1_ln1_b),
        l1_w_router.astype(F32), r1(l1_b_router))
    x2d = _moe(x1, idx, gates, rank, counts, l1_w1, l1_b1, l1_w2, l1_b2,
               r1(l1_ln2_w), r1(l1_ln2_b))
    return x2d.reshape(bsz, seq, d).astype(out_dtype)
```

```python
import functools

import jax
import jax.numpy as jnp
from jax import lax
from jax.experimental import pallas as pl
from jax.experimental.pallas import tpu as pltpu

F32 = jnp.float32
BF16 = jnp.bfloat16
I32 = jnp.int32

DEPTH = 2
DEEPNORM_ALPHA = (2.0 * DEPTH) ** 0.25
LN_EPS = 1e-5
RMS_EPS = 1e-6

A_HEADS = 4
A_DK = 128
POOL_WINDOWS = (2, 4, 8, 16)
POOL_HALO = 16
C_HEAD_DIM = 64
N_EXPERTS = 32
TOP_K = 4
SWIGLU_ALPHA = 1.702
SWIGLU_LIMIT = 7.0

HGRN_CHUNK = 16
ROW_DMA_UNROLL = 8
ROUTER_TM = 256
VMEM_LIMIT = 48 * 1024 * 1024


def _params(*sem):
    return pltpu.CompilerParams(dimension_semantics=sem, vmem_limit_bytes=VMEM_LIMIT)


def _dot(a, b):
    return jnp.dot(a, b, preferred_element_type=F32)


def _dot_nt(a, b):
    return lax.dot_general(a, b, (((1,), (1,)), ((), ())), preferred_element_type=F32)


def _dot_tn(a, b):
    return lax.dot_general(a, b, (((0,), (0,)), ((), ())), preferred_element_type=F32)


def _sigmoid(x):
    e = jnp.exp(-jnp.abs(x))
    r = 1.0 / (1.0 + e)
    return jnp.where(x >= 0, r, e * r)


def _layer_norm(y, w, b):
    mu = jnp.mean(y, axis=-1, keepdims=True)
    yc = y - mu
    var = jnp.mean(yc * yc, axis=-1, keepdims=True)
    return yc * lax.rsqrt(var + LN_EPS) * w + b


def _proj_kernel(x_ref, w_ref, o_ref):
    o_ref[...] = _dot(x_ref[...].astype(BF16), w_ref[...]).astype(o_ref.dtype)


def _proj(x2d, w, out_dtype, tm=512):
    m, k = x2d.shape
    n = w.shape[1]
    tm = min(tm, m)
    return pl.pallas_call(
        _proj_kernel,
        grid=(m // tm,),
        in_specs=[pl.BlockSpec((tm, k), lambda i: (i, 0)),
                  pl.BlockSpec((k, n), lambda i: (0, 0))],
        out_specs=pl.BlockSpec((tm, n), lambda i: (i, 0)),
        out_shape=jax.ShapeDtypeStruct((m, n), out_dtype),
        compiler_params=_params("arbitrary"),
        name="proj",
    )(x2d, w)


def _hgrn_kernel(q_ref, f_ref, i_ref, g_ref, lb_ref, nw_ref, ltri_ref, o_ref,
                 st_ref, cum_ref, kk_ref, oacc_ref, *, ts):
    c = HGRN_CHUNK

    @pl.when(pl.program_id(1) == 0)
    def _():
        st_ref[...] = jnp.zeros_like(st_ref)

    z = f_ref[...]
    lb = lb_ref[...]
    e = jnp.exp(-jnp.abs(z))
    r = 1.0 / (1.0 + e)
    er = e * r
    sig = jnp.where(z >= 0, r, er)
    sig_neg = jnp.where(z >= 0, er, r)
    logf = jnp.log(lb + (1.0 - lb) * sig)
    kk_ref[...] = (1.0 - lb) * sig_neg

    ltri = ltri_ref[...]
    x1 = logf.astype(BF16)
    r1 = logf - x1.astype(F32)
    x2 = r1.astype(BF16)
    x3 = (r1 - x2.astype(F32)).astype(BF16)
    cum_ref[...] = _dot(ltri, x1) + _dot(ltri, x2) + _dot(ltri, x3)

    t_iota = lax.broadcasted_iota(I32, (c, A_DK), 0)

    def chunk(ci, carry):
        r0 = pl.multiple_of(ci * c, c)
        for h in range(A_HEADS):
            cols = slice(h * A_DK, (h + 1) * A_DK)
            cum = cum_ref[pl.ds(r0, c), cols]
            q = q_ref[pl.ds(r0, c), cols]
            k = kk_ref[pl.ds(r0, c), cols]
            v = i_ref[pl.ds(r0, c), cols]
            last = cum[c - 1:c, :]
            st = st_ref[h]
            qd = (q * jnp.exp(cum)).astype(BF16)
            o = _dot_nt(qd, st.astype(BF16))
            for s in range(c):
                d = jnp.where(t_iota >= s, cum - cum[s:s + 1, :], -1e30)
                p = q * (k[s:s + 1, :] * jnp.exp(d))
                o = o + jnp.sum(p, axis=-1, keepdims=True) * v[s:s + 1, :]
            oacc_ref[pl.ds(r0, c), cols] = o
            kd = (k * jnp.exp(last - cum)).astype(BF16)
            st_ref[h] = jnp.exp(last) * st + _dot_tn(v.astype(BF16), kd)
        return carry

    lax.fori_loop(0, ts // c, chunk, 0)

    g = g_ref[...]
    gate = g * _sigmoid(g)
    for h in range(A_HEADS):
        cols = slice(h * A_DK, (h + 1) * A_DK)
        o = oacc_ref[:, cols]
        o = o * lax.rsqrt(jnp.mean(o * o, axis=-1, keepdims=True) + RMS_EPS) * nw_ref[...]
        o_ref[:, cols] = o * gate[:, cols]


def _hgrn(proj, lb, norm_w, bsz, seq, ts=256):
    ts = min(ts, seq)
    ns = seq // ts
    h = A_HEADS
    blk = jnp.arange(ts) // HGRN_CHUNK
    ltri = ((blk[:, None] == blk[None, :]) &
            (jnp.arange(ts)[:, None] >= jnp.arange(ts)[None, :])).astype(BF16)

    wd = h * A_DK

    def col(part):
        return pl.BlockSpec((ts, wd), lambda b, s, part=part: (b * ns + s, part))

    return pl.pallas_call(
        functools.partial(_hgrn_kernel, ts=ts),
        grid=(bsz, ns),
        in_specs=[col(0), col(1), col(2), col(3),
                  pl.BlockSpec((1, wd), lambda b, s: (0, 0)),
                  pl.BlockSpec((1, A_DK), lambda b, s: (0, 0)),
                  pl.BlockSpec((ts, ts), lambda b, s: (0, 0))],
        out_specs=pl.BlockSpec((ts, wd), lambda b, s: (b * ns + s, 0)),
        out_shape=jax.ShapeDtypeStruct((bsz * seq, wd), F32),
        scratch_shapes=[pltpu.VMEM((h, A_DK, A_DK), F32),
                        pltpu.VMEM((ts, wd), F32),
                        pltpu.VMEM((ts, wd), F32),
                        pltpu.VMEM((ts, wd), F32)],
        compiler_params=_params("arbitrary", "arbitrary"),
        name="hgrn2",
    )(proj, proj, proj, proj, lb, norm_w, ltri)


def _pool_kernel(v_ref, wp_ref, sc_ref, o_ref, buf_ref, *, ts):
    s = pl.program_id(1)
    hl = POOL_HALO

    @pl.when(s == 0)
    def _():
        buf_ref[0:hl, :] = jnp.zeros((hl, buf_ref.shape[1]), F32)

    x = v_ref[...]
    buf_ref[hl:hl + ts, :] = x
    pos = (s * ts + 1 + lax.broadcasted_iota(I32, (ts, 1), 0)).astype(F32)
    for g, win in enumerate(POOL_WINDOWS):
        lo, hi = g * 128, (g + 1) * 128
        xg = x[:, lo:hi]
        acc = xg
        for j in range(1, win):
            acc = acc + buf_ref[hl - j:hl - j + ts, lo:hi]
        pooled = acc / jnp.minimum(pos, float(win)) - xg
        o_ref[:, lo:hi] = _dot(pooled.astype(BF16), wp_ref[g]) * sc_ref[:, lo:hi]
    buf_ref[0:hl, :] = x[ts - hl:ts, :]


def _pool(proj, w_pool, pool_scale, bsz, seq, col_block, ts=512):
    ts = min(ts, seq)
    ns = seq // ts
    width = w_pool.shape[0] * w_pool.shape[1]
    return pl.pallas_call(
        functools.partial(_pool_kernel, ts=ts),
        grid=(bsz, ns),
        in_specs=[pl.BlockSpec((ts, width), lambda b, s: (b * ns + s, col_block)),
                  pl.BlockSpec(w_pool.shape, lambda b, s: (0, 0, 0)),
                  pl.BlockSpec((1, width), lambda b, s: (0, 0))],
        out_specs=pl.BlockSpec((ts, width), lambda b, s: (b * ns + s, 0)),
        out_shape=jax.ShapeDtypeStruct((bsz * seq, width), F32),
        scratch_shapes=[pltpu.VMEM((POOL_HALO + ts, width), F32)],
        compiler_params=_params("arbitrary", "arbitrary"),
        name="pool",
    )(proj, w_pool, pool_scale)


def _mix_router_kernel(*refs, n_parts, tm, fused_w1):
    parts = refs[:n_parts]
    wos = refs[n_parts:2 * n_parts]
    x_ref, lnw_ref, lnb_ref, wr_ref, br_ref, tri_ref = refs[2 * n_parts:2 * n_parts + 6]
    rest = refs[2 * n_parts + 6:]
    if fused_w1:
        w1_ref, perm_ref = rest[:2]
        rest = rest[2:]
        x1_ref, idx_ref, gate_ref, rank_ref, cnt_ref, w1g_ref, w1l_ref, run_ref = rest
        _deinterleave_kernel(w1_ref, perm_ref, w1g_ref, w1l_ref)
    else:
        x1_ref, idx_ref, gate_ref, rank_ref, cnt_ref, run_ref = rest

    @pl.when(pl.program_id(0) == 0)
    def _():
        run_ref[...] = jnp.zeros_like(run_ref)

    h = _dot(parts[0][...].astype(BF16), wos[0][...])
    for p_ref, w_ref in zip(parts[1:], wos[1:]):
        h = h + _dot(p_ref[...].astype(BF16), w_ref[...])
    x1 = _layer_norm(DEEPNORM_ALPHA * x_ref[...] + h, lnw_ref[...], lnb_ref[...])
    x1_ref[...] = x1

    ne = br_ref.shape[0]
    x_hi = x1.astype(BF16)
    x_mid = (x1 - x_hi.astype(F32)).astype(BF16)
    hi_mid = _dot_nt(wr_ref[...], x_hi)
    logits = hi_mid[:ne] + hi_mid[ne:] + _dot_nt(wr_ref[:ne, :], x_mid) + br_ref[...]
    e_iota = lax.broadcasted_iota(I32, (ne, tm), 0).astype(F32)
    k_iota = lax.broadcasted_iota(I32, (TOP_K, tm), 0)
    l = logits
    vals, idxs = [], []
    for _ in range(TOP_K):
        m = jnp.max(l, axis=0, keepdims=True)
        ik = jnp.min(jnp.where(l == m, e_iota, float(ne)), axis=0, keepdims=True)
        vals.append(m)
        idxs.append(ik)
        l = jnp.where(e_iota == ik, -jnp.inf, l)
    exps = [jnp.exp(v - vals[0]) for v in vals]
    tot = exps[0] + exps[1] + exps[2] + exps[3]

    onehot = jnp.zeros((ne, tm), F32)
    for ik in idxs:
        onehot = onehot + (e_iota == ik).astype(F32)
    base = _dot(onehot.astype(BF16), tri_ref[...]) + run_ref[...]

    idx_out = jnp.zeros((TOP_K, tm), I32)
    gate_out = jnp.zeros((TOP_K, tm), F32)
    rank_out = jnp.zeros((TOP_K, tm), I32)
    for kk in range(TOP_K):
        rk = jnp.sum(jnp.where(e_iota == idxs[kk], base, 0.0), axis=0, keepdims=True)
        idx_out = jnp.where(k_iota == kk, idxs[kk].astype(I32), idx_out)
        gate_out = jnp.where(k_iota == kk, exps[kk] / tot, gate_out)
        rank_out = jnp.where(k_iota == kk, rk.astype(I32), rank_out)
    idx_ref[...] = idx_out
    gate_ref[...] = gate_out
    rank_ref[...] = rank_out
    run_ref[...] = run_ref[...] + jnp.sum(onehot, axis=1, keepdims=True)
    cnt_ref[...] = run_ref[...].astype(I32)


def _mix_router(parts, w_o, x2d, ln_w, ln_b, w_router, b_router, w1, tm):
    t, d = x2d.shape
    tm = min(tm, t)
    n_parts = len(parts)
    wos, off = [], 0
    for p in parts:
        wos.append(lax.slice_in_dim(w_o, off, off + p.shape[1], axis=0))
        off += p.shape[1]
    tri = (jnp.arange(tm)[:, None] < jnp.arange(tm)[None, :]).astype(BF16)
    ne = w_router.shape[1]
    row = lambda i: (i, 0)
    col = lambda i: (0, i)
    fixed = lambda i: (0, 0)
    in_specs = ([pl.BlockSpec((tm, p.shape[1]), row) for p in parts] +
                [pl.BlockSpec(w.shape, fixed) for w in wos] +
                [pl.BlockSpec((tm, d), row),
                 pl.BlockSpec((1, d), fixed), pl.BlockSpec((1, d), fixed),
                 pl.BlockSpec((2 * ne, d), fixed), pl.BlockSpec((ne, 1), fixed),
                 pl.BlockSpec((tm, tm), fixed)])
    wr_t = w_router.T
    wr_hi = wr_t.astype(BF16)
    wr_mid = (wr_t - wr_hi.astype(F32)).astype(BF16)
    wr_split = jnp.concatenate([wr_hi, wr_mid], axis=0)
    out_specs = [pl.BlockSpec((tm, d), row),
                 pl.BlockSpec((TOP_K, tm), col),
                 pl.BlockSpec((TOP_K, tm), col),
                 pl.BlockSpec((TOP_K, tm), col),
                 pl.BlockSpec((ne, 1), fixed)]
    out_shape = [jax.ShapeDtypeStruct((t, d), F32),
                 jax.ShapeDtypeStruct((TOP_K, t), I32),
                 jax.ShapeDtypeStruct((TOP_K, t), F32),
                 jax.ShapeDtypeStruct((TOP_K, t), I32),
                 jax.ShapeDtypeStruct((ne, 1), I32)]
    operands = [*parts, *wos, x2d, ln_w, ln_b, wr_split, b_router.reshape(ne, 1), tri]

    steps = t // tm
    n_e, d_in, two_dff = w1.shape
    chunk = (n_e * two_dff) // steps
    fused_w1 = (chunk * steps == n_e * two_dff and chunk in (256, 512, 1024)
                and two_dff % chunk == 0)
    if fused_w1:
        per_e = two_dff // chunk
        wsel = lambda i: (i // per_e, 0, i % per_e)
        in_specs += [pl.BlockSpec((1, d_in, chunk), wsel), pl.BlockSpec((chunk, chunk), fixed)]
        out_specs += [pl.BlockSpec((1, d_in, chunk // 2), wsel)] * 2
        out_shape += [jax.ShapeDtypeStruct((n_e, d_in, two_dff // 2), BF16)] * 2
        operands += [w1, _deinterleave_perm(chunk)]

    res = pl.pallas_call(
        functools.partial(_mix_router_kernel, n_parts=n_parts, tm=tm, fused_w1=fused_w1),
        grid=(steps,),
        in_specs=in_specs,
        out_specs=out_specs,
        out_shape=out_shape,
        scratch_shapes=[pltpu.VMEM((ne, 1), F32)],
        compiler_params=_params("arbitrary"),
        name="mix_router",
    )(*operands)
    if not fused_w1:
        res = list(res) + list(_deinterleave(w1))
    return res


ROW_SUB = 8
ROW_LANE = 128


def _store_rows_as_tiles(ref, x, n):
    for s in range(ROW_SUB):
        ref[pl.ds(s, n, stride=ROW_SUB), :] = x[:, s * ROW_LANE:(s + 1) * ROW_LANE]


def _load_tiles_as_rows(ref, n):
    return jnp.concatenate([ref[pl.ds(s, n, stride=ROW_SUB), :] for s in range(ROW_SUB)], axis=1)


def _tile_of_row(ref, r):
    return ref.at[pl.ds(pl.multiple_of(r * ROW_SUB, ROW_SUB), ROW_SUB)]


def _dispatch_kernel(starts_ref, pc_ref, pos_ref, x_ref, xs_ref, zero_ref, stage_ref, sem, zsem,
                     *, tm, te):
    ne = starts_ref.shape[0]
    n_tiles = xs_ref.shape[0] // (te * ROW_SUB)
    step = pl.program_id(0)
    slot = step % 2

    def zero_tile(tile):
        rows = te * ROW_SUB
        return pltpu.make_async_copy(
            zero_ref, xs_ref.at[pl.ds(pl.multiple_of(tile * rows, rows), rows)], zsem)

    def wait_rows():
        for kk in range(TOP_K):
            pltpu.make_async_copy(stage_ref.at[0], xs_ref.at[pl.ds(0, tm * ROW_SUB)], sem).wait()

    def last_tile(e):
        return (starts_ref[e] + pc_ref[e]) // te - 1

    @pl.when(pl.program_id(0) == 0)
    def _():
        zero_ref[...] = jnp.zeros_like(zero_ref)
        n_used = (starts_ref[ne - 1] + pc_ref[ne - 1]) // te

        def start(e, carry):
            @pl.when(pc_ref[e] > 0)
            def _():
                zero_tile(last_tile(e)).start()
            return carry

        def wait(e, carry):
            @pl.when(pc_ref[e] > 0)
            def _():
                zero_tile(last_tile(e)).wait()
            return carry

        def start_tail(tile, carry):
            zero_tile(tile).start()
            return carry

        def wait_tail(tile, carry):
            zero_tile(tile).wait()
            return carry

        lax.fori_loop(0, ne, start, 0)
        lax.fori_loop(n_used, n_tiles, start_tail, 0)
        lax.fori_loop(0, ne, wait, 0)
        lax.fori_loop(n_used, n_tiles, wait_tail, 0)

    stage = stage_ref.at[slot]
    _store_rows_as_tiles(stage, x_ref[...], tm)

    @pl.when(step > 0)
    def _():
        wait_rows()

    def row(r, carry):
        for kk in range(TOP_K):
            p = pos_ref[0, 0, kk * tm + r]
            pltpu.make_async_copy(_tile_of_row(stage, r), _tile_of_row(xs_ref, p),
                                  sem).start(priority=kk % 2)
        return carry

    lax.fori_loop(0, tm, row, 0, unroll=ROW_DMA_UNROLL)

    @pl.when(step == pl.num_programs(0) - 1)
    def _():
        wait_rows()


def _dispatch(x1, pos_tiles, starts, pc, n_slots, tm, te):
    t, d = x1.shape
    assert d == ROW_SUB * ROW_LANE
    grid_spec = pltpu.PrefetchScalarGridSpec(
        num_scalar_prefetch=2,
        grid=(t // tm,),
        in_specs=[pl.BlockSpec((1, 1, tm * TOP_K), lambda i, *_: (i, 0, 0),
                               memory_space=pltpu.SMEM),
                  pl.BlockSpec((tm, d), lambda i, *_: (i, 0))],
        out_specs=pl.BlockSpec(memory_space=pl.ANY),
        scratch_shapes=[pltpu.VMEM((te * ROW_SUB, ROW_LANE), F32),
                        pltpu.VMEM((2, tm * ROW_SUB, ROW_LANE), F32),
                        pltpu.SemaphoreType.DMA,
                        pltpu.SemaphoreType.DMA],
    )
    return pl.pallas_call(
        functools.partial(_dispatch_kernel, tm=tm, te=te),
        grid_spec=grid_spec,
        out_shape=jax.ShapeDtypeStruct((n_slots * ROW_SUB, ROW_LANE), F32),
        compiler_params=_params("arbitrary"),
        name="moe_dispatch",
    )(starts, pc, pos_tiles, x1)


DEINT_CHUNK = 512


def _deinterleave_kernel(w_ref, p_ref, g_ref, l_ref):
    half = p_ref.shape[0] // 2
    r = _dot(w_ref[0].astype(BF16), p_ref[...])
    g_ref[0] = r[:, :half].astype(BF16)
    l_ref[0] = r[:, half:].astype(BF16)


def _deinterleave_perm(c):
    j = jnp.arange(c)
    return (j[None, :] == jnp.where(j % 2 == 0, j // 2, c // 2 + j // 2)[:, None]).astype(BF16)


def _deinterleave(w1):
    ne, d, two_dff = w1.shape
    c = DEINT_CHUNK
    half = c // 2
    perm = _deinterleave_perm(c)
    out = jax.ShapeDtypeStruct((ne, d, two_dff // 2), BF16)
    return pl.pallas_call(
        _deinterleave_kernel,
        grid=(ne, two_dff // c),
        in_specs=[pl.BlockSpec((1, d, c), lambda e, k: (e, 0, k)),
                  pl.BlockSpec((c, c), lambda e, k: (0, 0))],
        out_specs=[pl.BlockSpec((1, d, half), lambda e, k: (e, 0, k)),
                   pl.BlockSpec((1, d, half), lambda e, k: (e, 0, k))],
        out_shape=[out, out],
        compiler_params=_params("arbitrary", "arbitrary"),
        name="w1_deinterleave",
    )(w1, perm)


def _expert_kernel(te_ref, nu_ref, x_ref, w1g_ref, w1l_ref, b1g_ref, b1l_ref, w2_ref, b2_ref,
                   o_ref, *, te):
    @pl.when(pl.program_id(0) < nu_ref[0])
    def _():
        xb = _load_tiles_as_rows(x_ref, te).astype(BF16)
        hg = _dot(xb, w1g_ref[0]) + b1g_ref[0]
        hl = _dot(xb, w1l_ref[0]) + b1l_ref[0]
        glu = jnp.minimum(hg, SWIGLU_LIMIT)
        lin = jnp.clip(hl, -SWIGLU_LIMIT, SWIGLU_LIMIT)
        act = glu * _sigmoid(SWIGLU_ALPHA * glu) * (lin + 1.0)
        _store_rows_as_tiles(o_ref, _dot(act.astype(BF16), w2_ref[0]) + b2_ref[0], te)

    @pl.when(pl.program_id(0) >= nu_ref[0])
    def _():
        o_ref[...] = jnp.zeros_like(o_ref)


def _experts(xs, tile_expert, n_used, w1g, w1l, b1g, b1l, w2, b2, te):
    d = w1g.shape[1]
    dff = w1g.shape[2]
    n_tiles = xs.shape[0] // (te * ROW_SUB)
    blk = (te * ROW_SUB, ROW_LANE)

    def row(i, te_ref, nu_ref):
        return (jnp.minimum(i, nu_ref[0] - 1), 0)

    def wsel(i, te_ref, nu_ref):
        return (te_ref[i], 0, 0)

    grid_spec = pltpu.PrefetchScalarGridSpec(
        num_scalar_prefetch=2,
        grid=(n_tiles,),
        in_specs=[pl.BlockSpec(blk, row),
                  pl.BlockSpec((1, d, dff), wsel), pl.BlockSpec((1, d, dff), wsel),
                  pl.BlockSpec((1, 1, dff), wsel), pl.BlockSpec((1, 1, dff), wsel),
                  pl.BlockSpec((1, dff, d), wsel), pl.BlockSpec((1, 1, d), wsel)],
        out_specs=pl.BlockSpec(blk, lambda i, te_ref, nu_ref: (i, 0)),
    )
    return pl.pallas_call(
        functools.partial(_expert_kernel, te=te),
        grid_spec=grid_spec,
        out_shape=jax.ShapeDtypeStruct(xs.shape, F32),
        compiler_params=_params("arbitrary"),
        name="moe_experts",
    )(tile_expert, n_used, xs, w1g, w1l, b1g, b1l, w2, b2)


def _combine_kernel(pos_ref, pos_next_ref, x_ref, gate_ref, lnw_ref, lnb_ref, eo_ref, o_ref,
                    buf_ref, sem, *, tm):
    step = pl.program_id(0)
    slot = step % 2

    def gather(p_ref, slot_):
        def row(r, carry):
            for kk in range(TOP_K):
                p = p_ref[0, 0, kk * tm + r]
                pltpu.make_async_copy(_tile_of_row(eo_ref, p),
                                      _tile_of_row(buf_ref.at[slot_, kk], r),
                                      sem.at[slot_]).start(priority=kk % 2)
            return carry

        lax.fori_loop(0, tm, row, 0, unroll=ROW_DMA_UNROLL)

    @pl.when(step == 0)
    def _():
        gather(pos_ref, slot)

    @pl.when(step + 1 < pl.num_programs(0))
    def _():
        gather(pos_next_ref, 1 - slot)

    for kk in range(TOP_K):
        pltpu.make_async_copy(eo_ref.at[pl.ds(0, tm * ROW_SUB)], buf_ref.at[slot, kk],
                              sem.at[slot]).wait()

    g = gate_ref[...]
    y = g[:, 0:1] * _load_tiles_as_rows(buf_ref.at[slot, 0], tm)
    for kk in range(1, TOP_K):
        y = y + g[:, kk:kk + 1] * _load_tiles_as_rows(buf_ref.at[slot, kk], tm)
    o_ref[...] = _layer_norm(DEEPNORM_ALPHA * x_ref[...] + y, lnw_ref[...], lnb_ref[...])


def _combine(eo, pos_tiles, x1, gates, ln_w, ln_b, tm):
    t, d = x1.shape
    nt = t // tm
    row = lambda i: (i, 0)
    fixed = lambda i: (0, 0)
    pos_spec = lambda f: pl.BlockSpec((1, 1, tm * TOP_K), f, memory_space=pltpu.SMEM)
    return pl.pallas_call(
        functools.partial(_combine_kernel, tm=tm),
        grid=(nt,),
        in_specs=[pos_spec(lambda i: (i, 0, 0)),
                  pos_spec(lambda i: (jnp.minimum(i + 1, nt - 1), 0, 0)),
                  pl.BlockSpec((tm, d), row),
                  pl.BlockSpec((tm, TOP_K), row),
                  pl.BlockSpec((1, d), fixed), pl.BlockSpec((1, d), fixed),
                  pl.BlockSpec(memory_space=pl.ANY)],
        out_specs=pl.BlockSpec((tm, d), row),
        out_shape=jax.ShapeDtypeStruct((t, d), F32),
        scratch_shapes=[pltpu.VMEM((2, TOP_K, tm * ROW_SUB, ROW_LANE), F32),
                        pltpu.SemaphoreType.DMA((2,))],
        compiler_params=_params("arbitrary"),
        name="moe_combine",
    )(pos_tiles, pos_tiles, x1, gates, ln_w, ln_b, eo)


def _moe(x1, idx, gates, rank, counts, w1g, w1l, b1, w2, b2, ln_w, ln_b, tm=256, te=256):
    t, d = x1.shape
    tm = min(tm, t)
    ne = w1g.shape[0]
    counts = counts[:, 0]
    pc = ((counts + te - 1) // te) * te
    ends = jnp.cumsum(pc)
    starts = ends - pc
    n_tiles = (t * TOP_K) // te + ne
    n_used = (ends[-1] // te).astype(I32)
    tile_ids = jnp.arange(n_tiles, dtype=I32)
    te_raw = jnp.sum((tile_ids[:, None] >= (ends // te)[None, :]).astype(I32), axis=1)
    te_raw = jnp.minimum(te_raw, ne - 1)
    tile_expert = jnp.where(tile_ids < n_used, te_raw, jnp.take(te_raw, n_used - 1))
    e_ids = jnp.arange(ne, dtype=I32)[:, None, None]
    start_of = jnp.sum(jnp.where(idx[None] == e_ids, starts.astype(I32)[:, None, None], 0), axis=0)
    pos = start_of + rank
    pos_tiles = pos.reshape(TOP_K, t // tm, tm).transpose(1, 0, 2).reshape(t // tm, 1, TOP_K * tm)
    gates = gates.T

    b1g = b1[:, None, 0::2]
    b1l = b1[:, None, 1::2]

    xs = _dispatch(x1, pos_tiles, starts.astype(I32), pc.astype(I32), n_tiles * te, tm, te)
    eo = _experts(xs, tile_expert, n_used.reshape(1), w1g, w1l, b1g, b1l,
                  w2.astype(BF16), b2[:, None, :], te)
    return _combine(eo, pos_tiles, x1, gates, ln_w, ln_b, tm)


ATTN_GROUP = 4
ATTN_LOG_CUTOFF = -110.0


def _attn_kernel(q_ref, kt_ref, v_ref, nu_ref, o_ref, qs_ref, z0_ref, z1_ref, sp0_ref, sp1_ref,
                 w0_ref, w1_ref, acc_ref, carry_ref, *, tq):
    z_refs, sp_refs, w_refs = (z0_ref, z1_ref), (sp0_ref, sp1_ref), (w0_ref, w1_ref)
    i = pl.program_id(2)
    hd = C_HEAD_DIM
    g = ATTN_GROUP
    width = g * hd
    q = q_ref[...]
    lane = lax.broadcasted_iota(I32, (tq, width), 1)
    for h in range(g):
        in_head = (lane >= h * hd) & (lane < (h + 1) * hd)
        qs_ref[h * tq:(h + 1) * tq, :] = jnp.where(in_head, q, jnp.zeros_like(q))
    neg_u = nu_ref[...]
    row = lax.broadcasted_iota(I32, (g * tq, tq), 0) & (tq - 1)
    strict = lax.broadcasted_iota(I32, (g * tq, tq), 1) < row
    vmask = [(lane >= h * hd) & (lane < (h + 1) * hd) for h in range(g)]

    def key_rows(n):
        return pl.ds(pl.multiple_of(jnp.maximum(i - n, 0) * tq, tq), tq)

    def scores(n, slot, diag):
        z = _dot(qs_ref[...], kt_ref[jnp.maximum(i - n, 0)])
        z_refs[slot][...] = z
        zb = z.astype(BF16)
        sp = jnp.maximum(zb, 0) + jnp.log(1 + jnp.exp(-jnp.abs(zb)))
        if diag:
            sp = jnp.where(strict, sp, jnp.zeros_like(sp))
        sp_refs[slot][...] = sp

    def weights(slot, diag):
        suf = _dot(sp_refs[slot][...], neg_u)
        w = jnp.exp(z_refs[slot][...] + suf + carry_ref[...])
        if diag:
            w = jnp.where(strict, w, 0.0)
        w = w.astype(BF16)
        for h in range(g):
            w_refs[slot][:, h * tq:(h + 1) * tq] = w[h * tq:(h + 1) * tq, :]
        carry_ref[...] = carry_ref[...] + suf[:, 0:1]

    def output(n, slot, valid):
        vj = v_ref[key_rows(n), :]
        v_cat = jnp.concatenate([jnp.where(m, vj, jnp.zeros_like(vj)) for m in vmask], axis=0)
        pv = _dot(w_refs[slot][...], v_cat)
        if valid is not None:
            pv = jnp.where(valid, pv, 0.0)
        acc_ref[...] = acc_ref[...] + pv

    carry_ref[...] = jnp.zeros_like(carry_ref)
    acc_ref[...] = jnp.zeros_like(acc_ref)
    scores(0, 0, True)
    weights(0, True)
    scores(1, 1, False)
    output(0, 0, None)
    weights(1, False)

    n_groups = (i + 1) // 2
    carry_max = jnp.max(carry_ref[...])

    @pl.when((n_groups > 0) & (carry_max >= ATTN_LOG_CUTOFF))
    def _():
        scores(2, 0, False)

    def body(st):
        p, _ = st
        t = 3 + 2 * p
        output(t - 2, 1, None)
        weights(0, False)
        scores(t, 1, False)
        output(t - 1, 0, t - 1 <= i)
        weights(1, False)
        scores(t + 1, 0, False)
        return p + 1, jnp.max(carry_ref[...])

    def keep_going(st):
        p, carry_max = st
        return (p < n_groups) & (carry_max >= ATTN_LOG_CUTOFF)

    groups_done, _ = lax.while_loop(keep_going, body, (jnp.int32(0), carry_max))
    pending = 1 + 2 * groups_done
    output(pending, 1, pending <= i)
    o_ref[...] = acc_ref[...].astype(o_ref.dtype)


def _attention(qkv, bsz, seq, width, tq=256):
    tq = min(tq, seq)
    blk = ATTN_GROUP * C_HEAD_DIM
    ngroups = width // blk
    nq = seq // tq
    neg_u = -(jnp.arange(tq)[:, None] >= jnp.arange(tq)[None, :]).astype(BF16)
    qkv3 = qkv.reshape(bsz, seq, 3 * width)
    kt = qkv3[:, :, width:2 * width].reshape(bsz, nq, tq, width).transpose(0, 1, 3, 2)
    return pl.pallas_call(
        functools.partial(_attn_kernel, tq=tq),
        grid=(bsz, ngroups, nq),
        in_specs=[pl.BlockSpec((None, tq, blk), lambda b, p, i: (b, i, p)),
                  pl.BlockSpec((None, nq, blk, tq), lambda b, p, i: (b, 0, p, 0)),
                  pl.BlockSpec((None, seq, blk), lambda b, p, i: (b, 0, 2 * ngroups + p)),
                  pl.BlockSpec((tq, tq), lambda b, p, i: (0, 0))],
        out_specs=pl.BlockSpec((None, tq, blk), lambda b, p, i: (b, i, p)),
        out_shape=jax.ShapeDtypeStruct((bsz, seq, width), BF16),
        scratch_shapes=[pltpu.VMEM((ATTN_GROUP * tq, blk), BF16),
                        pltpu.VMEM((ATTN_GROUP * tq, tq), F32),
                        pltpu.VMEM((ATTN_GROUP * tq, tq), F32),
                        pltpu.VMEM((ATTN_GROUP * tq, tq), BF16),
                        pltpu.VMEM((ATTN_GROUP * tq, tq), BF16),
                        pltpu.VMEM((tq, ATTN_GROUP * tq), BF16),
                        pltpu.VMEM((tq, ATTN_GROUP * tq), BF16),
                        pltpu.VMEM((tq, blk), F32),
                        pltpu.VMEM((ATTN_GROUP * tq, 1), F32)],
        compiler_params=_params("arbitrary", "arbitrary", "arbitrary"),
        name="stickbreak_attn",
    )(qkv3, kt, qkv3, neg_u).reshape(bsz * seq, width)


def kernel(x, hgrn_lb_logits, l0_w_in, l0_a_norm_w, l0_w_pool, l0_pool_scale, l0_w_o, l0_ln1_w, l0_ln1_b, l0_w_router, l0_b_router, l0_w1, l0_b1, l0_w2, l0_b2, l0_ln2_w, l0_ln2_b, l1_w_in, l1_w_o, l1_ln1_w, l1_ln1_b, l1_w_router, l1_b_router, l1_w1, l1_b1, l1_w2, l1_b2, l1_ln2_w, l1_ln2_b):
    bsz, seq, d = x.shape
    out_dtype = x.dtype
    x2d = x.reshape(bsz * seq, d).astype(F32)
    r1 = lambda a: a.reshape(1, -1).astype(F32)

    a_width = A_HEADS * A_DK
    lb = jnp.cumsum(jax.nn.softmax(hgrn_lb_logits.astype(F32), axis=0), axis=0)[0]
    proj0 = _proj(x2d, l0_w_in.astype(BF16), F32)
    a_out = _hgrn(proj0, r1(lb), r1(l0_a_norm_w), bsz, seq)
    b_width = l0_w_pool.shape[0] * l0_w_pool.shape[1]
    b_out = _pool(proj0, l0_w_pool.astype(BF16), r1(l0_pool_scale), bsz, seq,
                  col_block=(4 * a_width) // b_width)
    x1, idx, gates, rank, counts, w1g, w1l = _mix_router(
        [a_out, b_out], l0_w_o.astype(BF16), x2d, r1(l0_ln1_w), r1(l0_ln1_b),
        l0_w_router.astype(F32), r1(l0_b_router), l0_w1, ROUTER_TM)
    x2d = _moe(x1, idx, gates, rank, counts, w1g, w1l, l0_b1, l0_w2, l0_b2,
               r1(l0_ln2_w), r1(l0_ln2_b))

    c_width = l1_w_in.shape[1] // 3
    qscale = jnp.concatenate([jnp.full((c_width,), C_HEAD_DIM ** -0.5, F32),
                              jnp.ones((2 * c_width,), F32)])
    qkv = _proj(x2d, (l1_w_in * qscale).astype(BF16), BF16)
    o = _attention(qkv, bsz, seq, c_width)
    x1, idx, gates, rank, counts, w1g, w1l = _mix_router(
        [o], l1_w_o.astype(BF16), x2d, r1(l1_ln1_w), r1(l1_ln1_b),
        l1_w_router.astype(F32), r1(l1_b_router), l1_w1, ROUTER_TM)
    x2d = _moe(x1, idx, gates, rank, counts, w1g, w1l, l1_b1, l1_w2, l1_b2,
               r1(l1_ln2_w), r1(l1_ln2_b))
    return x2d.reshape(bsz, seq, d).astype(out_dtype)
```

```python
import functools

import jax
import jax.numpy as jnp
from jax import lax
from jax.experimental import pallas as pl
from jax.experimental.pallas import tpu as pltpu

F32 = jnp.float32
BF16 = jnp.bfloat16
I32 = jnp.int32

DEPTH = 2
DEEPNORM_ALPHA = (2.0 * DEPTH) ** 0.25
LN_EPS = 1e-5
RMS_EPS = 1e-6

A_HEADS = 4
A_DK = 128
POOL_WINDOWS = (2, 4, 8, 16)
POOL_HALO = 16
C_HEAD_DIM = 64
N_EXPERTS = 32
TOP_K = 4
SWIGLU_ALPHA = 1.702
SWIGLU_LIMIT = 7.0

HGRN_CHUNK = 16
ROW_DMA_UNROLL = 8
ROUTER_TM = 256
VMEM_LIMIT = 48 * 1024 * 1024


def _params(*sem):
    return pltpu.CompilerParams(dimension_semantics=sem, vmem_limit_bytes=VMEM_LIMIT)


def _dot(a, b):
    return jnp.dot(a, b, preferred_element_type=F32)


def _dot_nt(a, b):
    return lax.dot_general(a, b, (((1,), (1,)), ((), ())), preferred_element_type=F32)


def _dot_tn(a, b):
    return lax.dot_general(a, b, (((0,), (0,)), ((), ())), preferred_element_type=F32)


def _sigmoid(x):
    e = jnp.exp(-jnp.abs(x))
    r = 1.0 / (1.0 + e)
    return jnp.where(x >= 0, r, e * r)


def _layer_norm(y, w, b):
    mu = jnp.mean(y, axis=-1, keepdims=True)
    yc = y - mu
    var = jnp.mean(yc * yc, axis=-1, keepdims=True)
    return yc * lax.rsqrt(var + LN_EPS) * w + b


def _proj_kernel(x_ref, w_ref, o_ref):
    o_ref[...] = _dot(x_ref[...].astype(BF16), w_ref[...]).astype(o_ref.dtype)


def _proj(x2d, w, out_dtype, tm=512):
    m, k = x2d.shape
    n = w.shape[1]
    tm = min(tm, m)
    return pl.pallas_call(
        _proj_kernel,
        grid=(m // tm,),
        in_specs=[pl.BlockSpec((tm, k), lambda i: (i, 0)),
                  pl.BlockSpec((k, n), lambda i: (0, 0))],
        out_specs=pl.BlockSpec((tm, n), lambda i: (i, 0)),
        out_shape=jax.ShapeDtypeStruct((m, n), out_dtype),
        compiler_params=_params("arbitrary"),
        name="proj",
    )(x2d, w)


def _hgrn_kernel(q_ref, f_ref, i_ref, g_ref, lb_ref, nw_ref, ltri_ref, o_ref,
                 st_ref, cum_ref, kk_ref, oacc_ref, *, ts):
    c = HGRN_CHUNK

    @pl.when(pl.program_id(1) == 0)
    def _():
        st_ref[...] = jnp.zeros_like(st_ref)

    z = f_ref[...]
    lb = lb_ref[...]
    e = jnp.exp(-jnp.abs(z))
    r = 1.0 / (1.0 + e)
    er = e * r
    sig = jnp.where(z >= 0, r, er)
    sig_neg = jnp.where(z >= 0, er, r)
    logf = jnp.log(lb + (1.0 - lb) * sig)
    kk_ref[...] = (1.0 - lb) * sig_neg

    ltri = ltri_ref[...]
    x1 = logf.astype(BF16)
    r1 = logf - x1.astype(F32)
    x2 = r1.astype(BF16)
    x3 = (r1 - x2.astype(F32)).astype(BF16)
    cum_ref[...] = _dot(ltri, x1) + _dot(ltri, x2) + _dot(ltri, x3)

    t_iota = lax.broadcasted_iota(I32, (c, A_DK), 0)

    def chunk(ci, carry):
        r0 = pl.multiple_of(ci * c, c)
        for h in range(A_HEADS):
            cols = slice(h * A_DK, (h + 1) * A_DK)
            cum = cum_ref[pl.ds(r0, c), cols]
            q = q_ref[pl.ds(r0, c), cols]
            k = kk_ref[pl.ds(r0, c), cols]
            v = i_ref[pl.ds(r0, c), cols]
            last = cum[c - 1:c, :]
            st = st_ref[h]
            qd = (q * jnp.exp(cum)).astype(BF16)
            o = _dot_nt(qd, st.astype(BF16))
            for s in range(c):
                d = jnp.where(t_iota >= s, cum - cum[s:s + 1, :], -1e30)
                p = q * (k[s:s + 1, :] * jnp.exp(d))
                o = o + jnp.sum(p, axis=-1, keepdims=True) * v[s:s + 1, :]
            oacc_ref[pl.ds(r0, c), cols] = o
            kd = (k * jnp.exp(last - cum)).astype(BF16)
            st_ref[h] = jnp.exp(last) * st + _dot_tn(v.astype(BF16), kd)
        return carry

    lax.fori_loop(0, ts // c, chunk, 0)

    g = g_ref[...]
    gate = g * _sigmoid(g)
    for h in range(A_HEADS):
        cols = slice(h * A_DK, (h + 1) * A_DK)
        o = oacc_ref[:, cols]
        o = o * lax.rsqrt(jnp.mean(o * o, axis=-1, keepdims=True) + RMS_EPS) * nw_ref[...]
        o_ref[:, cols] = o * gate[:, cols]


def _hgrn(proj, lb, norm_w, bsz, seq, ts=256):
    ts = min(ts, seq)
    ns = seq // ts
    h = A_HEADS
    blk = jnp.arange(ts) // HGRN_CHUNK
    ltri = ((blk[:, None] == blk[None, :]) &
            (jnp.arange(ts)[:, None] >= jnp.arange(ts)[None, :])).astype(BF16)

    wd = h * A_DK

    def col(part):
        return pl.BlockSpec((ts, wd), lambda b, s, part=part: (b * ns + s, part))

    return pl.pallas_call(
        functools.partial(_hgrn_kernel, ts=ts),
        grid=(bsz, ns),
        in_specs=[col(0), col(1), col(2), col(3),
                  pl.BlockSpec((1, wd), lambda b, s: (0, 0)),
                  pl.BlockSpec((1, A_DK), lambda b, s: (0, 0)),
                  pl.BlockSpec((ts, ts), lambda b, s: (0, 0))],
        out_specs=pl.BlockSpec((ts, wd), lambda b, s: (b * ns + s, 0)),
        out_shape=jax.ShapeDtypeStruct((bsz * seq, wd), F32),
        scratch_shapes=[pltpu.VMEM((h, A_DK, A_DK), F32),
                        pltpu.VMEM((ts, wd), F32),
                        pltpu.VMEM((ts, wd), F32),
                        pltpu.VMEM((ts, wd), F32)],
        compiler_params=_params("arbitrary", "arbitrary"),
        name="hgrn2",
    )(proj, proj, proj, proj, lb, norm_w, ltri)


def _pool_kernel(v_ref, wp_ref, sc_ref, o_ref, buf_ref, *, ts):
    s = pl.program_id(1)
    hl = POOL_HALO

    @pl.when(s == 0)
    def _():
        buf_ref[0:hl, :] = jnp.zeros((hl, buf_ref.shape[1]), F32)

    x = v_ref[...]
    buf_ref[hl:hl + ts, :] = x
    pos = (s * ts + 1 + lax.broadcasted_iota(I32, (ts, 1), 0)).astype(F32)
    for g, win in enumerate(POOL_WINDOWS):
        lo, hi = g * 128, (g + 1) * 128
        xg = x[:, lo:hi]
        acc = xg
        for j in range(1, win):
            acc = acc + buf_ref[hl - j:hl - j + ts, lo:hi]
        pooled = acc / jnp.minimum(pos, float(win)) - xg
        o_ref[:, lo:hi] = _dot(pooled.astype(BF16), wp_ref[g]) * sc_ref[:, lo:hi]
    buf_ref[0:hl, :] = x[ts - hl:ts, :]


def _pool(proj, w_pool, pool_scale, bsz, seq, col_block, ts=512):
    ts = min(ts, seq)
    ns = seq // ts
    width = w_pool.shape[0] * w_pool.shape[1]
    return pl.pallas_call(
        functools.partial(_pool_kernel, ts=ts),
        grid=(bsz, ns),
        in_specs=[pl.BlockSpec((ts, width), lambda b, s: (b * ns + s, col_block)),
                  pl.BlockSpec(w_pool.shape, lambda b, s: (0, 0, 0)),
                  pl.BlockSpec((1, width), lambda b, s: (0, 0))],
        out_specs=pl.BlockSpec((ts, width), lambda b, s: (b * ns + s, 0)),
        out_shape=jax.ShapeDtypeStruct((bsz * seq, width), F32),
        scratch_shapes=[pltpu.VMEM((POOL_HALO + ts, width), F32)],
        compiler_params=_params("arbitrary", "arbitrary"),
        name="pool",
    )(proj, w_pool, pool_scale)


def _mix_router_kernel(*refs, n_parts, tm, fused_w1):
    parts = refs[:n_parts]
    wos = refs[n_parts:2 * n_parts]
    x_ref, lnw_ref, lnb_ref, wr_ref, br_ref, tri_ref = refs[2 * n_parts:2 * n_parts + 6]
    rest = refs[2 * n_parts + 6:]
    if fused_w1:
        w1_ref, perm_ref, w2_ref = rest[:3]
        rest = rest[3:]
        x1_ref, idx_ref, gate_ref, rank_ref, cnt_ref, w1g_ref, w1l_ref, w2b_ref, run_ref = rest
        _deinterleave_kernel(w1_ref, perm_ref, w1g_ref, w1l_ref)
        w2b_ref[...] = w2_ref[...].astype(BF16)
    else:
        x1_ref, idx_ref, gate_ref, rank_ref, cnt_ref, run_ref = rest

    @pl.when(pl.program_id(0) == 0)
    def _():
        run_ref[...] = jnp.zeros_like(run_ref)

    h = _dot(parts[0][...].astype(BF16), wos[0][...])
    for p_ref, w_ref in zip(parts[1:], wos[1:]):
        h = h + _dot(p_ref[...].astype(BF16), w_ref[...])
    x1 = _layer_norm(DEEPNORM_ALPHA * x_ref[...] + h, lnw_ref[...], lnb_ref[...])
    x1_ref[...] = x1

    ne = br_ref.shape[0]
    x_hi = x1.astype(BF16)
    x_mid = (x1 - x_hi.astype(F32)).astype(BF16)
    hi_mid = _dot_nt(wr_ref[...], x_hi)
    logits = hi_mid[:ne] + hi_mid[ne:] + _dot_nt(wr_ref[:ne, :], x_mid) + br_ref[...]
    e_iota = lax.broadcasted_iota(I32, (ne, tm), 0).astype(F32)
    k_iota = lax.broadcasted_iota(I32, (TOP_K, tm), 0)
    l = logits
    vals, idxs = [], []
    for _ in range(TOP_K):
        m = jnp.max(l, axis=0, keepdims=True)
        ik = jnp.min(jnp.where(l == m, e_iota, float(ne)), axis=0, keepdims=True)
        vals.append(m)
        idxs.append(ik)
        l = jnp.where(e_iota == ik, -jnp.inf, l)
    exps = [jnp.exp(v - vals[0]) for v in vals]
    tot = exps[0] + exps[1] + exps[2] + exps[3]

    onehot = jnp.zeros((ne, tm), F32)
    for ik in idxs:
        onehot = onehot + (e_iota == ik).astype(F32)
    base = _dot(onehot.astype(BF16), tri_ref[...]) + run_ref[...]

    idx_out = jnp.zeros((TOP_K, tm), I32)
    gate_out = jnp.zeros((TOP_K, tm), F32)
    rank_out = jnp.zeros((TOP_K, tm), I32)
    for kk in range(TOP_K):
        rk = jnp.sum(jnp.where(e_iota == idxs[kk], base, 0.0), axis=0, keepdims=True)
        idx_out = jnp.where(k_iota == kk, idxs[kk].astype(I32), idx_out)
        gate_out = jnp.where(k_iota == kk, exps[kk] / tot, gate_out)
        rank_out = jnp.where(k_iota == kk, rk.astype(I32), rank_out)
    idx_ref[...] = idx_out
    gate_ref[...] = gate_out
    rank_ref[...] = rank_out
    run_ref[...] = run_ref[...] + jnp.sum(onehot, axis=1, keepdims=True)
    cnt_ref[...] = run_ref[...].astype(I32)


def _mix_router(parts, w_o, x2d, ln_w, ln_b, w_router, b_router, w1, w2, tm):
    t, d = x2d.shape
    tm = min(tm, t)
    n_parts = len(parts)
    wos, off = [], 0
    for p in parts:
        wos.append(lax.slice_in_dim(w_o, off, off + p.shape[1], axis=0))
        off += p.shape[1]
    tri = (jnp.arange(tm)[:, None] < jnp.arange(tm)[None, :]).astype(BF16)
    ne = w_router.shape[1]
    row = lambda i: (i, 0)
    col = lambda i: (0, i)
    fixed = lambda i: (0, 0)
    in_specs = ([pl.BlockSpec((tm, p.shape[1]), row) for p in parts] +
                [pl.BlockSpec(w.shape, fixed) for w in wos] +
                [pl.BlockSpec((tm, d), row),
                 pl.BlockSpec((1, d), fixed), pl.BlockSpec((1, d), fixed),
                 pl.BlockSpec((2 * ne, d), fixed), pl.BlockSpec((ne, 1), fixed),
                 pl.BlockSpec((tm, tm), fixed)])
    wr_t = w_router.T
    wr_hi = wr_t.astype(BF16)
    wr_mid = (wr_t - wr_hi.astype(F32)).astype(BF16)
    wr_split = jnp.concatenate([wr_hi, wr_mid], axis=0)
    out_specs = [pl.BlockSpec((tm, d), row),
                 pl.BlockSpec((TOP_K, tm), col),
                 pl.BlockSpec((TOP_K, tm), col),
                 pl.BlockSpec((TOP_K, tm), col),
                 pl.BlockSpec((ne, 1), fixed)]
    out_shape = [jax.ShapeDtypeStruct((t, d), F32),
                 jax.ShapeDtypeStruct((TOP_K, t), I32),
                 jax.ShapeDtypeStruct((TOP_K, t), F32),
                 jax.ShapeDtypeStruct((TOP_K, t), I32),
                 jax.ShapeDtypeStruct((ne, 1), I32)]
    operands = [*parts, *wos, x2d, ln_w, ln_b, wr_split, b_router.reshape(ne, 1), tri]

    steps = t // tm
    n_e, d_in, two_dff = w1.shape
    chunk = (n_e * two_dff) // steps
    fused_w1 = (chunk * steps == n_e * two_dff and chunk in (256, 512, 1024)
                and two_dff % chunk == 0)
    if fused_w1:
        per_e = two_dff // chunk
        wsel = lambda i: (i // per_e, 0, i % per_e)
        w2_rows = w2.shape[1] // per_e
        w2sel = lambda i: (i // per_e, i % per_e, 0)
        in_specs += [pl.BlockSpec((1, d_in, chunk), wsel), pl.BlockSpec((chunk, chunk), fixed),
                     pl.BlockSpec((1, w2_rows, w2.shape[2]), w2sel)]
        out_specs += [pl.BlockSpec((1, d_in, chunk // 2), wsel)] * 2
        out_specs += [pl.BlockSpec((1, w2_rows, w2.shape[2]), w2sel)]
        out_shape += [jax.ShapeDtypeStruct((n_e, d_in, two_dff // 2), BF16)] * 2
        out_shape += [jax.ShapeDtypeStruct(w2.shape, BF16)]
        operands += [w1, _deinterleave_perm(chunk), w2]

    res = pl.pallas_call(
        functools.partial(_mix_router_kernel, n_parts=n_parts, tm=tm, fused_w1=fused_w1),
        grid=(steps,),
        in_specs=in_specs,
        out_specs=out_specs,
        out_shape=out_shape,
        scratch_shapes=[pltpu.VMEM((ne, 1), F32)],
        compiler_params=_params("arbitrary"),
        name="mix_router",
    )(*operands)
    if not fused_w1:
        res = list(res) + list(_deinterleave(w1)) + [w2.astype(BF16)]
    return res


ROW_SUB = 8
ROW_LANE = 128


def _store_rows_as_tiles(ref, x, n):
    for s in range(ROW_SUB):
        ref[pl.ds(s, n, stride=ROW_SUB), :] = x[:, s * ROW_LANE:(s + 1) * ROW_LANE]


def _load_tiles_as_rows(ref, n):
    return jnp.concatenate([ref[pl.ds(s, n, stride=ROW_SUB), :] for s in range(ROW_SUB)], axis=1)


def _tile_of_row(ref, r):
    return ref.at[pl.ds(pl.multiple_of(r * ROW_SUB, ROW_SUB), ROW_SUB)]


def _dispatch_kernel(starts_ref, pc_ref, pos_ref, x_ref, xs_ref, zero_ref, stage_ref, sem, zsem,
                     *, tm, te):
    ne = starts_ref.shape[0]
    n_tiles = xs_ref.shape[0] // (te * ROW_SUB)
    step = pl.program_id(0)
    slot = step % 2

    def zero_tile(tile):
        rows = te * ROW_SUB
        return pltpu.make_async_copy(
            zero_ref, xs_ref.at[pl.ds(pl.multiple_of(tile * rows, rows), rows)], zsem)

    def wait_rows():
        for kk in range(TOP_K):
            pltpu.make_async_copy(stage_ref.at[0], xs_ref.at[pl.ds(0, tm * ROW_SUB)], sem).wait()

    def last_tile(e):
        return (starts_ref[e] + pc_ref[e]) // te - 1

    @pl.when(pl.program_id(0) == 0)
    def _():
        zero_ref[...] = jnp.zeros_like(zero_ref)
        n_used = (starts_ref[ne - 1] + pc_ref[ne - 1]) // te

        def start(e, carry):
            @pl.when(pc_ref[e] > 0)
            def _():
                zero_tile(last_tile(e)).start()
            return carry

        def wait(e, carry):
            @pl.when(pc_ref[e] > 0)
            def _():
                zero_tile(last_tile(e)).wait()
            return carry

        def start_tail(tile, carry):
            zero_tile(tile).start()
            return carry

        def wait_tail(tile, carry):
            zero_tile(tile).wait()
            return carry

        lax.fori_loop(0, ne, start, 0)
        lax.fori_loop(n_used, n_tiles, start_tail, 0)
        lax.fori_loop(0, ne, wait, 0)
        lax.fori_loop(n_used, n_tiles, wait_tail, 0)

    stage = stage_ref.at[slot]
    _store_rows_as_tiles(stage, x_ref[...], tm)

    @pl.when(step > 0)
    def _():
        wait_rows()

    def row(r, carry):
        for kk in range(TOP_K):
            p = pos_ref[0, 0, kk * tm + r]
            pltpu.make_async_copy(_tile_of_row(stage, r), _tile_of_row(xs_ref, p),
                                  sem).start(priority=kk % 2)
        return carry

    lax.fori_loop(0, tm, row, 0, unroll=ROW_DMA_UNROLL)

    @pl.when(step == pl.num_programs(0) - 1)
    def _():
        wait_rows()


def _dispatch(x1, pos_tiles, starts, pc, n_slots, tm, te):
    t, d = x1.shape
    assert d == ROW_SUB * ROW_LANE
    grid_spec = pltpu.PrefetchScalarGridSpec(
        num_scalar_prefetch=2,
        grid=(t // tm,),
        in_specs=[pl.BlockSpec((1, 1, tm * TOP_K), lambda i, *_: (i, 0, 0),
                               memory_space=pltpu.SMEM),
                  pl.BlockSpec((tm, d), lambda i, *_: (i, 0))],
        out_specs=pl.BlockSpec(memory_space=pl.ANY),
        scratch_shapes=[pltpu.VMEM((te * ROW_SUB, ROW_LANE), F32),
                        pltpu.VMEM((2, tm * ROW_SUB, ROW_LANE), F32),
                        pltpu.SemaphoreType.DMA,
                        pltpu.SemaphoreType.DMA],
    )
    return pl.pallas_call(
        functools.partial(_dispatch_kernel, tm=tm, te=te),
        grid_spec=grid_spec,
        out_shape=jax.ShapeDtypeStruct((n_slots * ROW_SUB, ROW_LANE), F32),
        compiler_params=_params("arbitrary"),
        name="moe_dispatch",
    )(starts, pc, pos_tiles, x1)


DEINT_CHUNK = 512


def _deinterleave_kernel(w_ref, p_ref, g_ref, l_ref):
    half = p_ref.shape[0] // 2
    r = _dot(w_ref[0].astype(BF16), p_ref[...])
    g_ref[0] = r[:, :half].astype(BF16)
    l_ref[0] = r[:, half:].astype(BF16)


def _deinterleave_perm(c):
    j = jnp.arange(c)
    return (j[None, :] == jnp.where(j % 2 == 0, j // 2, c // 2 + j // 2)[:, None]).astype(BF16)


def _deinterleave(w1):
    ne, d, two_dff = w1.shape
    c = DEINT_CHUNK
    half = c // 2
    perm = _deinterleave_perm(c)
    out = jax.ShapeDtypeStruct((ne, d, two_dff // 2), BF16)
    return pl.pallas_call(
        _deinterleave_kernel,
        grid=(ne, two_dff // c),
        in_specs=[pl.BlockSpec((1, d, c), lambda e, k: (e, 0, k)),
                  pl.BlockSpec((c, c), lambda e, k: (0, 0))],
        out_specs=[pl.BlockSpec((1, d, half), lambda e, k: (e, 0, k)),
                   pl.BlockSpec((1, d, half), lambda e, k: (e, 0, k))],
        out_shape=[out, out],
        compiler_params=_params("arbitrary", "arbitrary"),
        name="w1_deinterleave",
    )(w1, perm)


def _expert_kernel(te_ref, nu_ref, x_ref, w1g_ref, w1l_ref, b1g_ref, b1l_ref, w2_ref, b2_ref,
                   o_ref, *, te):
    @pl.when(pl.program_id(0) < nu_ref[0])
    def _():
        xb = _load_tiles_as_rows(x_ref, te).astype(BF16)
        hg = _dot(xb, w1g_ref[0]) + b1g_ref[0]
        hl = _dot(xb, w1l_ref[0]) + b1l_ref[0]
        glu = jnp.minimum(hg, SWIGLU_LIMIT)
        lin = jnp.clip(hl, -SWIGLU_LIMIT, SWIGLU_LIMIT)
        act = glu * _sigmoid(SWIGLU_ALPHA * glu) * (lin + 1.0)
        _store_rows_as_tiles(o_ref, _dot(act.astype(BF16), w2_ref[0]) + b2_ref[0], te)

    @pl.when(pl.program_id(0) >= nu_ref[0])
    def _():
        o_ref[...] = jnp.zeros_like(o_ref)


def _experts(xs, tile_expert, n_used, w1g, w1l, b1g, b1l, w2, b2, te):
    d = w1g.shape[1]
    dff = w1g.shape[2]
    n_tiles = xs.shape[0] // (te * ROW_SUB)
    blk = (te * ROW_SUB, ROW_LANE)

    def row(i, te_ref, nu_ref):
        return (jnp.minimum(i, nu_ref[0] - 1), 0)

    def wsel(i, te_ref, nu_ref):
        return (te_ref[i], 0, 0)

    grid_spec = pltpu.PrefetchScalarGridSpec(
        num_scalar_prefetch=2,
        grid=(n_tiles,),
        in_specs=[pl.BlockSpec(blk, row),
                  pl.BlockSpec((1, d, dff), wsel), pl.BlockSpec((1, d, dff), wsel),
                  pl.BlockSpec((1, 1, dff), wsel), pl.BlockSpec((1, 1, dff), wsel),
                  pl.BlockSpec((1, dff, d), wsel), pl.BlockSpec((1, 1, d), wsel)],
        out_specs=pl.BlockSpec(blk, lambda i, te_ref, nu_ref: (i, 0)),
    )
    return pl.pallas_call(
        functools.partial(_expert_kernel, te=te),
        grid_spec=grid_spec,
        out_shape=jax.ShapeDtypeStruct(xs.shape, F32),
        compiler_params=_params("arbitrary"),
        name="moe_experts",
    )(tile_expert, n_used, xs, w1g, w1l, b1g, b1l, w2, b2)


def _combine_kernel(pos_ref, pos_next_ref, x_ref, gate_ref, lnw_ref, lnb_ref, eo_ref, o_ref,
                    buf_ref, sem, *, tm):
    step = pl.program_id(0)
    slot = step % 2

    def gather(p_ref, slot_):
        def row(r, carry):
            for kk in range(TOP_K):
                p = p_ref[0, 0, kk * tm + r]
                pltpu.make_async_copy(_tile_of_row(eo_ref, p),
                                      _tile_of_row(buf_ref.at[slot_, kk], r),
                                      sem.at[slot_]).start(priority=kk % 2)
            return carry

        lax.fori_loop(0, tm, row, 0, unroll=ROW_DMA_UNROLL)

    @pl.when(step == 0)
    def _():
        gather(pos_ref, slot)

    @pl.when(step + 1 < pl.num_programs(0))
    def _():
        gather(pos_next_ref, 1 - slot)

    for kk in range(TOP_K):
        pltpu.make_async_copy(eo_ref.at[pl.ds(0, tm * ROW_SUB)], buf_ref.at[slot, kk],
                              sem.at[slot]).wait()

    g = gate_ref[...]
    y = g[:, 0:1] * _load_tiles_as_rows(buf_ref.at[slot, 0], tm)
    for kk in range(1, TOP_K):
        y = y + g[:, kk:kk + 1] * _load_tiles_as_rows(buf_ref.at[slot, kk], tm)
    o_ref[...] = _layer_norm(DEEPNORM_ALPHA * x_ref[...] + y, lnw_ref[...], lnb_ref[...])


def _combine(eo, pos_tiles, x1, gates, ln_w, ln_b, tm):
    t, d = x1.shape
    nt = t // tm
    row = lambda i: (i, 0)
    fixed = lambda i: (0, 0)
    pos_spec = lambda f: pl.BlockSpec((1, 1, tm * TOP_K), f, memory_space=pltpu.SMEM)
    return pl.pallas_call(
        functools.partial(_combine_kernel, tm=tm),
        grid=(nt,),
        in_specs=[pos_spec(lambda i: (i, 0, 0)),
                  pos_spec(lambda i: (jnp.minimum(i + 1, nt - 1), 0, 0)),
                  pl.BlockSpec((tm, d), row),
                  pl.BlockSpec((tm, TOP_K), row),
                  pl.BlockSpec((1, d), fixed), pl.BlockSpec((1, d), fixed),
                  pl.BlockSpec(memory_space=pl.ANY)],
        out_specs=pl.BlockSpec((tm, d), row),
        out_shape=jax.ShapeDtypeStruct((t, d), F32),
        scratch_shapes=[pltpu.VMEM((2, TOP_K, tm * ROW_SUB, ROW_LANE), F32),
                        pltpu.SemaphoreType.DMA((2,))],
        compiler_params=_params("arbitrary"),
        name="moe_combine",
    )(pos_tiles, pos_tiles, x1, gates, ln_w, ln_b, eo)


def _moe(x1, idx, gates, rank, counts, w1g, w1l, b1, w2, b2, ln_w, ln_b, tm=256, te=256):
    t, d = x1.shape
    tm = min(tm, t)
    ne = w1g.shape[0]
    counts = counts[:, 0]
    pc = ((counts + te - 1) // te) * te
    ends = jnp.cumsum(pc)
    starts = ends - pc
    n_tiles = (t * TOP_K) // te + ne
    n_used = (ends[-1] // te).astype(I32)
    tile_ids = jnp.arange(n_tiles, dtype=I32)
    te_raw = jnp.sum((tile_ids[:, None] >= (ends // te)[None, :]).astype(I32), axis=1)
    te_raw = jnp.minimum(te_raw, ne - 1)
    tile_expert = jnp.where(tile_ids < n_used, te_raw, jnp.take(te_raw, n_used - 1))
    e_ids = jnp.arange(ne, dtype=I32)[:, None, None]
    start_of = jnp.sum(jnp.where(idx[None] == e_ids, starts.astype(I32)[:, None, None], 0), axis=0)
    pos = start_of + rank
    pos_tiles = pos.reshape(TOP_K, t // tm, tm).transpose(1, 0, 2).reshape(t // tm, 1, TOP_K * tm)
    gates = gates.T

    b1g = b1[:, None, 0::2]
    b1l = b1[:, None, 1::2]

    xs = _dispatch(x1, pos_tiles, starts.astype(I32), pc.astype(I32), n_tiles * te, tm, te)
    eo = _experts(xs, tile_expert, n_used.reshape(1), w1g, w1l, b1g, b1l,
                  w2, b2[:, None, :], te)
    return _combine(eo, pos_tiles, x1, gates, ln_w, ln_b, tm)


ATTN_GROUP = 4
ATTN_LOG_CUTOFF = -110.0


def _attn_kernel(q_ref, kt_ref, v_ref, nu_ref, o_ref, qs_ref, z0_ref, z1_ref, sp0_ref, sp1_ref,
                 w0_ref, w1_ref, acc_ref, carry_ref, *, tq):
    z_refs, sp_refs, w_refs = (z0_ref, z1_ref), (sp0_ref, sp1_ref), (w0_ref, w1_ref)
    i = pl.program_id(2)
    hd = C_HEAD_DIM
    g = ATTN_GROUP
    width = g * hd
    q = q_ref[...]
    lane = lax.broadcasted_iota(I32, (tq, width), 1)
    for h in range(g):
        in_head = (lane >= h * hd) & (lane < (h + 1) * hd)
        qs_ref[h * tq:(h + 1) * tq, :] = jnp.where(in_head, q, jnp.zeros_like(q))
    neg_u = nu_ref[...]
    row = lax.broadcasted_iota(I32, (g * tq, tq), 0) & (tq - 1)
    strict = lax.broadcasted_iota(I32, (g * tq, tq), 1) < row
    vmask = [(lane >= h * hd) & (lane < (h + 1) * hd) for h in range(g)]

    def key_rows(n):
        return pl.ds(pl.multiple_of(jnp.maximum(i - n, 0) * tq, tq), tq)

    def scores(n, slot, diag):
        z = _dot(qs_ref[...], kt_ref[jnp.maximum(i - n, 0)])
        z_refs[slot][...] = z
        zb = z.astype(BF16)
        sp = jnp.maximum(zb, 0) + jnp.log(1 + jnp.exp(-jnp.abs(zb)))
        if diag:
            sp = jnp.where(strict, sp, jnp.zeros_like(sp))
        sp_refs[slot][...] = sp

    def weights(slot, diag):
        suf = _dot(sp_refs[slot][...], neg_u)
        w = jnp.exp(z_refs[slot][...] + suf + carry_ref[...])
        if diag:
            w = jnp.where(strict, w, 0.0)
        w = w.astype(BF16)
        for h in range(g):
            w_refs[slot][:, h * tq:(h + 1) * tq] = w[h * tq:(h + 1) * tq, :]
        carry_ref[...] = carry_ref[...] + suf[:, 0:1]

    def output(n, slot, valid):
        vj = v_ref[key_rows(n), :]
        v_cat = jnp.concatenate([jnp.where(m, vj, jnp.zeros_like(vj)) for m in vmask], axis=0)
        pv = _dot(w_refs[slot][...], v_cat)
        if valid is not None:
            pv = jnp.where(valid, pv, 0.0)
        acc_ref[...] = acc_ref[...] + pv

    carry_ref[...] = jnp.zeros_like(carry_ref)
    acc_ref[...] = jnp.zeros_like(acc_ref)
    scores(0, 0, True)
    weights(0, True)
    scores(1, 1, False)
    output(0, 0, None)
    weights(1, False)

    n_groups = (i + 1) // 2
    carry_max = jnp.max(carry_ref[...])

    @pl.when((n_groups > 0) & (carry_max >= ATTN_LOG_CUTOFF))
    def _():
        scores(2, 0, False)

    def body(st):
        p, _ = st
        t = 3 + 2 * p
        output(t - 2, 1, None)
        weights(0, False)
        scores(t, 1, False)
        output(t - 1, 0, t - 1 <= i)
        weights(1, False)
        scores(t + 1, 0, False)
        return p + 1, jnp.max(carry_ref[...])

    def keep_going(st):
        p, carry_max = st
        return (p < n_groups) & (carry_max >= ATTN_LOG_CUTOFF)

    groups_done, _ = lax.while_loop(keep_going, body, (jnp.int32(0), carry_max))
    pending = 1 + 2 * groups_done
    output(pending, 1, pending <= i)
    o_ref[...] = acc_ref[...].astype(o_ref.dtype)


def _attention(qkv, bsz, seq, width, tq=256):
    tq = min(tq, seq)
    blk = ATTN_GROUP * C_HEAD_DIM
    ngroups = width // blk
    nq = seq // tq
    neg_u = -(jnp.arange(tq)[:, None] >= jnp.arange(tq)[None, :]).astype(BF16)
    qkv3 = qkv.reshape(bsz, seq, 3 * width)
    kt = qkv3[:, :, width:2 * width].reshape(bsz, nq, tq, width).transpose(0, 1, 3, 2)
    return pl.pallas_call(
        functools.partial(_attn_kernel, tq=tq),
        grid=(bsz, ngroups, nq),
        in_specs=[pl.BlockSpec((None, tq, blk), lambda b, p, i: (b, i, p)),
                  pl.BlockSpec((None, nq, blk, tq), lambda b, p, i: (b, 0, p, 0)),
                  pl.BlockSpec((None, seq, blk), lambda b, p, i: (b, 0, 2 * ngroups + p)),
                  pl.BlockSpec((tq, tq), lambda b, p, i: (0, 0))],
        out_specs=pl.BlockSpec((None, tq, blk), lambda b, p, i: (b, i, p)),
        out_shape=jax.ShapeDtypeStruct((bsz, seq, width), BF16),
        scratch_shapes=[pltpu.VMEM((ATTN_GROUP * tq, blk), BF16),
                        pltpu.VMEM((ATTN_GROUP * tq, tq), F32),
                        pltpu.VMEM((ATTN_GROUP * tq, tq), F32),
                        pltpu.VMEM((ATTN_GROUP * tq, tq), BF16),
                        pltpu.VMEM((ATTN_GROUP * tq, tq), BF16),
                        pltpu.VMEM((tq, ATTN_GROUP * tq), BF16),
                        pltpu.VMEM((tq, ATTN_GROUP * tq), BF16),
                        pltpu.VMEM((tq, blk), F32),
                        pltpu.VMEM((ATTN_GROUP * tq, 1), F32)],
        compiler_params=_params("arbitrary", "arbitrary", "arbitrary"),
        name="stickbreak_attn",
    )(qkv3, kt, qkv3, neg_u).reshape(bsz * seq, width)


def kernel(x, hgrn_lb_logits, l0_w_in, l0_a_norm_w, l0_w_pool, l0_pool_scale, l0_w_o, l0_ln1_w, l0_ln1_b, l0_w_router, l0_b_router, l0_w1, l0_b1, l0_w2, l0_b2, l0_ln2_w, l0_ln2_b, l1_w_in, l1_w_o, l1_ln1_w, l1_ln1_b, l1_w_router, l1_b_router, l1_w1, l1_b1, l1_w2, l1_b2, l1_ln2_w, l1_ln2_b):
    bsz, seq, d = x.shape
    out_dtype = x.dtype
    x2d = x.reshape(bsz * seq, d).astype(F32)
    r1 = lambda a: a.reshape(1, -1).astype(F32)

    a_width = A_HEADS * A_DK
    lb = jnp.cumsum(jax.nn.softmax(hgrn_lb_logits.astype(F32), axis=0), axis=0)[0]
    proj0 = _proj(x2d, l0_w_in.astype(BF16), F32)
    a_out = _hgrn(proj0, r1(lb), r1(l0_a_norm_w), bsz, seq)
    b_width = l0_w_pool.shape[0] * l0_w_pool.shape[1]
    b_out = _pool(proj0, l0_w_pool.astype(BF16), r1(l0_pool_scale), bsz, seq,
                  col_block=(4 * a_width) // b_width)
    x1, idx, gates, rank, counts, w1g, w1l, w2b = _mix_router(
        [a_out, b_out], l0_w_o.astype(BF16), x2d, r1(l0_ln1_w), r1(l0_ln1_b),
        l0_w_router.astype(F32), r1(l0_b_router), l0_w1, l0_w2, ROUTER_TM)
    x2d = _moe(x1, idx, gates, rank, counts, w1g, w1l, l0_b1, w2b, l0_b2,
               r1(l0_ln2_w), r1(l0_ln2_b))

    c_width = l1_w_in.shape[1] // 3
    qscale = jnp.concatenate([jnp.full((c_width,), C_HEAD_DIM ** -0.5, F32),
                              jnp.ones((2 * c_width,), F32)])
    qkv = _proj(x2d, (l1_w_in * qscale).astype(BF16), BF16)
    o = _attention(qkv, bsz, seq, c_width)
    x1, idx, gates, rank, counts, w1g, w1l, w2b = _mix_router(
        [o], l1_w_o.astype(BF16), x2d, r1(l1_ln1_w), r1(l1_ln1_b),
        l1_w_router.astype(F32), r1(l1_b_router), l1_w1, l1_w2, ROUTER_TM)
    x2d = _moe(x1, idx, gates, rank, counts, w1g, w1l, l1_b1, w2b, l1_b2,
               r1(l1_ln2_w), r1(l1_ln2_b))
    return x2d.reshape(bsz, seq, d).astype(out_dtype)
```

```python
import functools

import jax
import jax.numpy as jnp
from jax import lax
from jax.experimental import pallas as pl
from jax.experimental.pallas import tpu as pltpu

F32 = jnp.float32
BF16 = jnp.bfloat16
I32 = jnp.int32

DEPTH = 2
DEEPNORM_ALPHA = (2.0 * DEPTH) ** 0.25
LN_EPS = 1e-5
RMS_EPS = 1e-6

A_HEADS = 4
A_DK = 128
POOL_WINDOWS = (2, 4, 8, 16)
POOL_HALO = 16
C_HEAD_DIM = 64
N_EXPERTS = 32
TOP_K = 4
SWIGLU_ALPHA = 1.702
SWIGLU_LIMIT = 7.0

HGRN_CHUNK = 16
ROW_DMA_UNROLL = 8
ROUTER_TM = 256
VMEM_LIMIT = 48 * 1024 * 1024


def _params(*sem):
    return pltpu.CompilerParams(dimension_semantics=sem, vmem_limit_bytes=VMEM_LIMIT)


def _dot(a, b):
    return jnp.dot(a, b, preferred_element_type=F32)


def _dot_nt(a, b):
    return lax.dot_general(a, b, (((1,), (1,)), ((), ())), preferred_element_type=F32)


def _dot_tn(a, b):
    return lax.dot_general(a, b, (((0,), (0,)), ((), ())), preferred_element_type=F32)


def _sigmoid(x):
    e = jnp.exp(-jnp.abs(x))
    r = 1.0 / (1.0 + e)
    return jnp.where(x >= 0, r, e * r)


def _layer_norm(y, w, b):
    mu = jnp.mean(y, axis=-1, keepdims=True)
    yc = y - mu
    var = jnp.mean(yc * yc, axis=-1, keepdims=True)
    return yc * lax.rsqrt(var + LN_EPS) * w + b


def _proj_kernel(x_ref, w_ref, o_ref):
    o_ref[...] = _dot(x_ref[...].astype(BF16), w_ref[...]).astype(o_ref.dtype)


def _proj(x2d, w, out_dtype, tm=512):
    m, k = x2d.shape
    n = w.shape[1]
    tm = min(tm, m)
    return pl.pallas_call(
        _proj_kernel,
        grid=(m // tm,),
        in_specs=[pl.BlockSpec((tm, k), lambda i: (i, 0)),
                  pl.BlockSpec((k, n), lambda i: (0, 0))],
        out_specs=pl.BlockSpec((tm, n), lambda i: (i, 0)),
        out_shape=jax.ShapeDtypeStruct((m, n), out_dtype),
        compiler_params=_params("arbitrary"),
        name="proj",
    )(x2d, w)


def _hgrn_kernel(q_ref, f_ref, i_ref, g_ref, lb_ref, nw_ref, ltri_ref, o_ref,
                 st_ref, cum_ref, kk_ref, oacc_ref, *, ts):
    c = HGRN_CHUNK

    @pl.when(pl.program_id(1) == 0)
    def _():
        st_ref[...] = jnp.zeros_like(st_ref)

    z = f_ref[...]
    lb = lb_ref[...]
    e = jnp.exp(-jnp.abs(z))
    r = 1.0 / (1.0 + e)
    er = e * r
    sig = jnp.where(z >= 0, r, er)
    sig_neg = jnp.where(z >= 0, er, r)
    logf = jnp.log(lb + (1.0 - lb) * sig)
    kk_ref[...] = (1.0 - lb) * sig_neg

    ltri = ltri_ref[...]
    x1 = logf.astype(BF16)
    r1 = logf - x1.astype(F32)
    x2 = r1.astype(BF16)
    x3 = (r1 - x2.astype(F32)).astype(BF16)
    cum_ref[...] = _dot(ltri, x1) + _dot(ltri, x2) + _dot(ltri, x3)

    t_iota = lax.broadcasted_iota(I32, (c, A_DK), 0)

    def chunk(ci, carry):
        r0 = pl.multiple_of(ci * c, c)
        for h in range(A_HEADS):
            cols = slice(h * A_DK, (h + 1) * A_DK)
            cum = cum_ref[pl.ds(r0, c), cols]
            q = q_ref[pl.ds(r0, c), cols]
            k = kk_ref[pl.ds(r0, c), cols]
            v = i_ref[pl.ds(r0, c), cols]
            last = cum[c - 1:c, :]
            st = st_ref[h]
            qd = (q * jnp.exp(cum)).astype(BF16)
            o = _dot_nt(qd, st.astype(BF16))
            for s in range(c):
                d = jnp.where(t_iota >= s, cum - cum[s:s + 1, :], -1e30)
                p = q * (k[s:s + 1, :] * jnp.exp(d))
                o = o + jnp.sum(p, axis=-1, keepdims=True) * v[s:s + 1, :]
            oacc_ref[pl.ds(r0, c), cols] = o
            kd = (k * jnp.exp(last - cum)).astype(BF16)
            st_ref[h] = jnp.exp(last) * st + _dot_tn(v.astype(BF16), kd)
        return carry

    lax.fori_loop(0, ts // c, chunk, 0)

    g = g_ref[...]
    gate = g * _sigmoid(g)
    for h in range(A_HEADS):
        cols = slice(h * A_DK, (h + 1) * A_DK)
        o = oacc_ref[:, cols]
        o = o * lax.rsqrt(jnp.mean(o * o, axis=-1, keepdims=True) + RMS_EPS) * nw_ref[...]
        o_ref[:, cols] = (o * gate[:, cols]).astype(o_ref.dtype)


def _hgrn(proj, lb, norm_w, bsz, seq, ts=256):
    ts = min(ts, seq)
    ns = seq // ts
    h = A_HEADS
    blk = jnp.arange(ts) // HGRN_CHUNK
    ltri = ((blk[:, None] == blk[None, :]) &
            (jnp.arange(ts)[:, None] >= jnp.arange(ts)[None, :])).astype(BF16)

    wd = h * A_DK

    def col(part):
        return pl.BlockSpec((ts, wd), lambda b, s, part=part: (b * ns + s, part))

    return pl.pallas_call(
        functools.partial(_hgrn_kernel, ts=ts),
        grid=(bsz, ns),
        in_specs=[col(0), col(1), col(2), col(3),
                  pl.BlockSpec((1, wd), lambda b, s: (0, 0)),
                  pl.BlockSpec((1, A_DK), lambda b, s: (0, 0)),
                  pl.BlockSpec((ts, ts), lambda b, s: (0, 0))],
        out_specs=pl.BlockSpec((ts, wd), lambda b, s: (b * ns + s, 0)),
        out_shape=jax.ShapeDtypeStruct((bsz * seq, wd), BF16),
        scratch_shapes=[pltpu.VMEM((h, A_DK, A_DK), F32),
                        pltpu.VMEM((ts, wd), F32),
                        pltpu.VMEM((ts, wd), F32),
                        pltpu.VMEM((ts, wd), F32)],
        compiler_params=_params("arbitrary", "arbitrary"),
        name="hgrn2",
    )(proj, proj, proj, proj, lb, norm_w, ltri)


def _pool_kernel(v_ref, wp_ref, sc_ref, o_ref, buf_ref, *, ts):
    s = pl.program_id(1)
    hl = POOL_HALO

    @pl.when(s == 0)
    def _():
        buf_ref[0:hl, :] = jnp.zeros((hl, buf_ref.shape[1]), F32)

    x = v_ref[...]
    buf_ref[hl:hl + ts, :] = x
    pos = (s * ts + 1 + lax.broadcasted_iota(I32, (ts, 1), 0)).astype(F32)
    for g, win in enumerate(POOL_WINDOWS):
        lo, hi = g * 128, (g + 1) * 128
        xg = x[:, lo:hi]
        acc = xg
        for j in range(1, win):
            acc = acc + buf_ref[hl - j:hl - j + ts, lo:hi]
        pooled = acc / jnp.minimum(pos, float(win)) - xg
        out = _dot(pooled.astype(BF16), wp_ref[g]) * sc_ref[:, lo:hi]
        o_ref[:, lo:hi] = out.astype(o_ref.dtype)
    buf_ref[0:hl, :] = x[ts - hl:ts, :]


def _pool(proj, w_pool, pool_scale, bsz, seq, col_block, ts=512):
    ts = min(ts, seq)
    ns = seq // ts
    width = w_pool.shape[0] * w_pool.shape[1]
    return pl.pallas_call(
        functools.partial(_pool_kernel, ts=ts),
        grid=(bsz, ns),
        in_specs=[pl.BlockSpec((ts, width), lambda b, s: (b * ns + s, col_block)),
                  pl.BlockSpec(w_pool.shape, lambda b, s: (0, 0, 0)),
                  pl.BlockSpec((1, width), lambda b, s: (0, 0))],
        out_specs=pl.BlockSpec((ts, width), lambda b, s: (b * ns + s, 0)),
        out_shape=jax.ShapeDtypeStruct((bsz * seq, width), BF16),
        scratch_shapes=[pltpu.VMEM((POOL_HALO + ts, width), F32)],
        compiler_params=_params("arbitrary", "arbitrary"),
        name="pool",
    )(proj, w_pool, pool_scale)


def _mix_router_kernel(*refs, n_parts, tm, fused_w1):
    parts = refs[:n_parts]
    wos = refs[n_parts:2 * n_parts]
    x_ref, lnw_ref, lnb_ref, wr_ref, br_ref, tri_ref = refs[2 * n_parts:2 * n_parts + 6]
    rest = refs[2 * n_parts + 6:]
    if fused_w1:
        w1_ref, perm_ref, w2_ref = rest[:3]
        rest = rest[3:]
        x1_ref, idx_ref, gate_ref, rank_ref, cnt_ref, w1g_ref, w1l_ref, w2b_ref, run_ref = rest
        _deinterleave_kernel(w1_ref, perm_ref, w1g_ref, w1l_ref)
        w2b_ref[...] = w2_ref[...].astype(BF16)
    else:
        x1_ref, idx_ref, gate_ref, rank_ref, cnt_ref, run_ref = rest

    @pl.when(pl.program_id(0) == 0)
    def _():
        run_ref[...] = jnp.zeros_like(run_ref)

    h = _dot(parts[0][...].astype(BF16), wos[0][...])
    for p_ref, w_ref in zip(parts[1:], wos[1:]):
        h = h + _dot(p_ref[...].astype(BF16), w_ref[...])
    x1 = _layer_norm(DEEPNORM_ALPHA * x_ref[...] + h, lnw_ref[...], lnb_ref[...])
    x1_ref[...] = x1

    ne = br_ref.shape[0]
    x_hi = x1.astype(BF16)
    x_mid = (x1 - x_hi.astype(F32)).astype(BF16)
    hi_mid = _dot_nt(wr_ref[...], x_hi)
    logits = hi_mid[:ne] + hi_mid[ne:] + _dot_nt(wr_ref[:ne, :], x_mid) + br_ref[...]
    e_iota = lax.broadcasted_iota(I32, (ne, tm), 0).astype(F32)
    k_iota = lax.broadcasted_iota(I32, (TOP_K, tm), 0)
    l = logits
    vals, idxs = [], []
    for _ in range(TOP_K):
        m = jnp.max(l, axis=0, keepdims=True)
        ik = jnp.min(jnp.where(l == m, e_iota, float(ne)), axis=0, keepdims=True)
        vals.append(m)
        idxs.append(ik)
        l = jnp.where(e_iota == ik, -jnp.inf, l)
    exps = [jnp.exp(v - vals[0]) for v in vals]
    tot = exps[0] + exps[1] + exps[2] + exps[3]

    onehot = jnp.zeros((ne, tm), F32)
    for ik in idxs:
        onehot = onehot + (e_iota == ik).astype(F32)
    base = _dot(onehot.astype(BF16), tri_ref[...]) + run_ref[...]

    idx_out = jnp.zeros((TOP_K, tm), I32)
    gate_out = jnp.zeros((TOP_K, tm), F32)
    rank_out = jnp.zeros((TOP_K, tm), I32)
    for kk in range(TOP_K):
        rk = jnp.sum(jnp.where(e_iota == idxs[kk], base, 0.0), axis=0, keepdims=True)
        idx_out = jnp.where(k_iota == kk, idxs[kk].astype(I32), idx_out)
        gate_out = jnp.where(k_iota == kk, exps[kk] / tot, gate_out)
        rank_out = jnp.where(k_iota == kk, rk.astype(I32), rank_out)
    idx_ref[...] = idx_out
    gate_ref[...] = gate_out
    rank_ref[...] = rank_out
    run_ref[...] = run_ref[...] + jnp.sum(onehot, axis=1, keepdims=True)
    cnt_ref[...] = run_ref[...].astype(I32)


def _mix_router(parts, w_o, x2d, ln_w, ln_b, w_router, b_router, w1, w2, tm):
    t, d = x2d.shape
    tm = min(tm, t)
    n_parts = len(parts)
    wos, off = [], 0
    for p in parts:
        wos.append(lax.slice_in_dim(w_o, off, off + p.shape[1], axis=0))
        off += p.shape[1]
    tri = (jnp.arange(tm)[:, None] < jnp.arange(tm)[None, :]).astype(BF16)
    ne = w_router.shape[1]
    row = lambda i: (i, 0)
    col = lambda i: (0, i)
    fixed = lambda i: (0, 0)
    in_specs = ([pl.BlockSpec((tm, p.shape[1]), row) for p in parts] +
                [pl.BlockSpec(w.shape, fixed) for w in wos] +
                [pl.BlockSpec((tm, d), row),
                 pl.BlockSpec((1, d), fixed), pl.BlockSpec((1, d), fixed),
                 pl.BlockSpec((2 * ne, d), fixed), pl.BlockSpec((ne, 1), fixed),
                 pl.BlockSpec((tm, tm), fixed)])
    wr_t = w_router.T
    wr_hi = wr_t.astype(BF16)
    wr_mid = (wr_t - wr_hi.astype(F32)).astype(BF16)
    wr_split = jnp.concatenate([wr_hi, wr_mid], axis=0)
    out_specs = [pl.BlockSpec((tm, d), row),
                 pl.BlockSpec((TOP_K, tm), col),
                 pl.BlockSpec((TOP_K, tm), col),
                 pl.BlockSpec((TOP_K, tm), col),
                 pl.BlockSpec((ne, 1), fixed)]
    out_shape = [jax.ShapeDtypeStruct((t, d), F32),
                 jax.ShapeDtypeStruct((TOP_K, t), I32),
                 jax.ShapeDtypeStruct((TOP_K, t), F32),
                 jax.ShapeDtypeStruct((TOP_K, t), I32),
                 jax.ShapeDtypeStruct((ne, 1), I32)]
    operands = [*parts, *wos, x2d, ln_w, ln_b, wr_split, b_router.reshape(ne, 1), tri]

    steps = t // tm
    n_e, d_in, two_dff = w1.shape
    chunk = (n_e * two_dff) // steps
    fused_w1 = (chunk * steps == n_e * two_dff and chunk in (256, 512, 1024)
                and two_dff % chunk == 0)
    if fused_w1:
        per_e = two_dff // chunk
        wsel = lambda i: (i // per_e, 0, i % per_e)
        w2_rows = w2.shape[1] // per_e
        w2sel = lambda i: (i // per_e, i % per_e, 0)
        in_specs += [pl.BlockSpec((1, d_in, chunk), wsel), pl.BlockSpec((chunk, chunk), fixed),
                     pl.BlockSpec((1, w2_rows, w2.shape[2]), w2sel)]
        out_specs += [pl.BlockSpec((1, d_in, chunk // 2), wsel)] * 2
        out_specs += [pl.BlockSpec((1, w2_rows, w2.shape[2]), w2sel)]
        out_shape += [jax.ShapeDtypeStruct((n_e, d_in, two_dff // 2), BF16)] * 2
        out_shape += [jax.ShapeDtypeStruct(w2.shape, BF16)]
        operands += [w1, _deinterleave_perm(chunk), w2]

    res = pl.pallas_call(
        functools.partial(_mix_router_kernel, n_parts=n_parts, tm=tm, fused_w1=fused_w1),
        grid=(steps,),
        in_specs=in_specs,
        out_specs=out_specs,
        out_shape=out_shape,
        scratch_shapes=[pltpu.VMEM((ne, 1), F32)],
        compiler_params=_params("arbitrary"),
        name="mix_router",
    )(*operands)
    if not fused_w1:
        res = list(res) + list(_deinterleave(w1)) + [w2.astype(BF16)]
    return res


ROW_SUB = 8
ROW_LANE = 128


def _store_rows_as_tiles(ref, x, n):
    for s in range(ROW_SUB):
        ref[pl.ds(s, n, stride=ROW_SUB), :] = x[:, s * ROW_LANE:(s + 1) * ROW_LANE]


def _load_tiles_as_rows(ref, n):
    return jnp.concatenate([ref[pl.ds(s, n, stride=ROW_SUB), :] for s in range(ROW_SUB)], axis=1)


def _tile_of_row(ref, r):
    return ref.at[pl.ds(pl.multiple_of(r * ROW_SUB, ROW_SUB), ROW_SUB)]


def _dispatch_kernel(starts_ref, pc_ref, pos_ref, x_ref, xs_ref, zero_ref, stage_ref, sem, zsem,
                     *, tm, te):
    ne = starts_ref.shape[0]
    n_tiles = xs_ref.shape[0] // (te * ROW_SUB)
    step = pl.program_id(0)
    slot = step % 2

    def zero_tile(tile):
        rows = te * ROW_SUB
        return pltpu.make_async_copy(
            zero_ref, xs_ref.at[pl.ds(pl.multiple_of(tile * rows, rows), rows)], zsem)

    def wait_rows():
        for kk in range(TOP_K):
            pltpu.make_async_copy(stage_ref.at[0], xs_ref.at[pl.ds(0, tm * ROW_SUB)], sem).wait()

    def last_tile(e):
        return (starts_ref[e] + pc_ref[e]) // te - 1

    @pl.when(pl.program_id(0) == 0)
    def _():
        zero_ref[...] = jnp.zeros_like(zero_ref)
        n_used = (starts_ref[ne - 1] + pc_ref[ne - 1]) // te

        def start(e, carry):
            @pl.when(pc_ref[e] > 0)
            def _():
                zero_tile(last_tile(e)).start()
            return carry

        def wait(e, carry):
            @pl.when(pc_ref[e] > 0)
            def _():
                zero_tile(last_tile(e)).wait()
            return carry

        def start_tail(tile, carry):
            zero_tile(tile).start()
            return carry

        def wait_tail(tile, carry):
            zero_tile(tile).wait()
            return carry

        lax.fori_loop(0, ne, start, 0)
        lax.fori_loop(n_used, n_tiles, start_tail, 0)
        lax.fori_loop(0, ne, wait, 0)
        lax.fori_loop(n_used, n_tiles, wait_tail, 0)

    stage = stage_ref.at[slot]
    _store_rows_as_tiles(stage, x_ref[...], tm)

    @pl.when(step > 0)
    def _():
        wait_rows()

    def row(r, carry):
        for kk in range(TOP_K):
            p = pos_ref[0, 0, kk * tm + r]
            pltpu.make_async_copy(_tile_of_row(stage, r), _tile_of_row(xs_ref, p),
                                  sem).start(priority=kk % 2)
        return carry

    lax.fori_loop(0, tm, row, 0, unroll=ROW_DMA_UNROLL)

    @pl.when(step == pl.num_programs(0) - 1)
    def _():
        wait_rows()


def _dispatch(x1, pos_tiles, starts, pc, n_slots, tm, te):
    t, d = x1.shape
    assert d == ROW_SUB * ROW_LANE
    grid_spec = pltpu.PrefetchScalarGridSpec(
        num_scalar_prefetch=2,
        grid=(t // tm,),
        in_specs=[pl.BlockSpec((1, 1, tm * TOP_K), lambda i, *_: (i, 0, 0),
                               memory_space=pltpu.SMEM),
                  pl.BlockSpec((tm, d), lambda i, *_: (i, 0))],
        out_specs=pl.BlockSpec(memory_space=pl.ANY),
        scratch_shapes=[pltpu.VMEM((te * ROW_SUB, ROW_LANE), F32),
                        pltpu.VMEM((2, tm * ROW_SUB, ROW_LANE), F32),
                        pltpu.SemaphoreType.DMA,
                        pltpu.SemaphoreType.DMA],
    )
    return pl.pallas_call(
        functools.partial(_dispatch_kernel, tm=tm, te=te),
        grid_spec=grid_spec,
        out_shape=jax.ShapeDtypeStruct((n_slots * ROW_SUB, ROW_LANE), F32),
        compiler_params=_params("arbitrary"),
        name="moe_dispatch",
    )(starts, pc, pos_tiles, x1)


DEINT_CHUNK = 512


def _deinterleave_kernel(w_ref, p_ref, g_ref, l_ref):
    half = p_ref.shape[0] // 2
    r = _dot(w_ref[0].astype(BF16), p_ref[...])
    g_ref[0] = r[:, :half].astype(BF16)
    l_ref[0] = r[:, half:].astype(BF16)


def _deinterleave_perm(c):
    j = jnp.arange(c)
    return (j[None, :] == jnp.where(j % 2 == 0, j // 2, c // 2 + j // 2)[:, None]).astype(BF16)


def _deinterleave(w1):
    ne, d, two_dff = w1.shape
    c = DEINT_CHUNK
    half = c // 2
    perm = _deinterleave_perm(c)
    out = jax.ShapeDtypeStruct((ne, d, two_dff // 2), BF16)
    return pl.pallas_call(
        _deinterleave_kernel,
        grid=(ne, two_dff // c),
        in_specs=[pl.BlockSpec((1, d, c), lambda e, k: (e, 0, k)),
                  pl.BlockSpec((c, c), lambda e, k: (0, 0))],
        out_specs=[pl.BlockSpec((1, d, half), lambda e, k: (e, 0, k)),
                   pl.BlockSpec((1, d, half), lambda e, k: (e, 0, k))],
        out_shape=[out, out],
        compiler_params=_params("arbitrary", "arbitrary"),
        name="w1_deinterleave",
    )(w1, perm)


def _expert_kernel(te_ref, nu_ref, x_ref, w1g_ref, w1l_ref, b1g_ref, b1l_ref, w2_ref, b2_ref,
                   o_ref, *, te):
    @pl.when(pl.program_id(0) < nu_ref[0])
    def _():
        xb = _load_tiles_as_rows(x_ref, te).astype(BF16)
        hg = _dot(xb, w1g_ref[0]) + b1g_ref[0]
        hl = _dot(xb, w1l_ref[0]) + b1l_ref[0]
        glu = jnp.minimum(hg, SWIGLU_LIMIT)
        lin = jnp.clip(hl, -SWIGLU_LIMIT, SWIGLU_LIMIT)
        act = glu * _sigmoid(SWIGLU_ALPHA * glu) * (lin + 1.0)
        _store_rows_as_tiles(o_ref, _dot(act.astype(BF16), w2_ref[0]) + b2_ref[0], te)

    @pl.when(pl.program_id(0) >= nu_ref[0])
    def _():
        o_ref[...] = jnp.zeros_like(o_ref)


def _experts(xs, tile_expert, n_used, w1g, w1l, b1g, b1l, w2, b2, te):
    d = w1g.shape[1]
    dff = w1g.shape[2]
    n_tiles = xs.shape[0] // (te * ROW_SUB)
    blk = (te * ROW_SUB, ROW_LANE)

    def row(i, te_ref, nu_ref):
        return (jnp.minimum(i, nu_ref[0] - 1), 0)

    def wsel(i, te_ref, nu_ref):
        return (te_ref[i], 0, 0)

    grid_spec = pltpu.PrefetchScalarGridSpec(
        num_scalar_prefetch=2,
        grid=(n_tiles,),
        in_specs=[pl.BlockSpec(blk, row),
                  pl.BlockSpec((1, d, dff), wsel), pl.BlockSpec((1, d, dff), wsel),
                  pl.BlockSpec((1, 1, dff), wsel), pl.BlockSpec((1, 1, dff), wsel),
                  pl.BlockSpec((1, dff, d), wsel), pl.BlockSpec((1, 1, d), wsel)],
        out_specs=pl.BlockSpec(blk, lambda i, te_ref, nu_ref: (i, 0)),
    )
    return pl.pallas_call(
        functools.partial(_expert_kernel, te=te),
        grid_spec=grid_spec,
        out_shape=jax.ShapeDtypeStruct(xs.shape, F32),
        compiler_params=_params("arbitrary"),
        name="moe_experts",
    )(tile_expert, n_used, xs, w1g, w1l, b1g, b1l, w2, b2)


def _combine_kernel(pos_ref, pos_next_ref, x_ref, gate_ref, lnw_ref, lnb_ref, eo_ref, o_ref,
                    buf_ref, sem, *, tm):
    step = pl.program_id(0)
    slot = step % 2

    def gather(p_ref, slot_):
        def row(r, carry):
            for kk in range(TOP_K):
                p = p_ref[0, 0, kk * tm + r]
                pltpu.make_async_copy(_tile_of_row(eo_ref, p),
                                      _tile_of_row(buf_ref.at[slot_, kk], r),
                                      sem.at[slot_]).start(priority=kk % 2)
            return carry

        lax.fori_loop(0, tm, row, 0, unroll=ROW_DMA_UNROLL)

    @pl.when(step == 0)
    def _():
        gather(pos_ref, slot)

    @pl.when(step + 1 < pl.num_programs(0))
    def _():
        gather(pos_next_ref, 1 - slot)

    for kk in range(TOP_K):
        pltpu.make_async_copy(eo_ref.at[pl.ds(0, tm * ROW_SUB)], buf_ref.at[slot, kk],
                              sem.at[slot]).wait()

    g = gate_ref[...]
    y = g[:, 0:1] * _load_tiles_as_rows(buf_ref.at[slot, 0], tm)
    for kk in range(1, TOP_K):
        y = y + g[:, kk:kk + 1] * _load_tiles_as_rows(buf_ref.at[slot, kk], tm)
    o_ref[...] = _layer_norm(DEEPNORM_ALPHA * x_ref[...] + y, lnw_ref[...], lnb_ref[...])


def _combine(eo, pos_tiles, x1, gates, ln_w, ln_b, tm):
    t, d = x1.shape
    nt = t // tm
    row = lambda i: (i, 0)
    fixed = lambda i: (0, 0)
    pos_spec = lambda f: pl.BlockSpec((1, 1, tm * TOP_K), f, memory_space=pltpu.SMEM)
    return pl.pallas_call(
        functools.partial(_combine_kernel, tm=tm),
        grid=(nt,),
        in_specs=[pos_spec(lambda i: (i, 0, 0)),
                  pos_spec(lambda i: (jnp.minimum(i + 1, nt - 1), 0, 0)),
                  pl.BlockSpec((tm, d), row),
                  pl.BlockSpec((tm, TOP_K), row),
                  pl.BlockSpec((1, d), fixed), pl.BlockSpec((1, d), fixed),
                  pl.BlockSpec(memory_space=pl.ANY)],
        out_specs=pl.BlockSpec((tm, d), row),
        out_shape=jax.ShapeDtypeStruct((t, d), F32),
        scratch_shapes=[pltpu.VMEM((2, TOP_K, tm * ROW_SUB, ROW_LANE), F32),
                        pltpu.SemaphoreType.DMA((2,))],
        compiler_params=_params("arbitrary"),
        name="moe_combine",
    )(pos_tiles, pos_tiles, x1, gates, ln_w, ln_b, eo)


def _moe(x1, idx, gates, rank, counts, w1g, w1l, b1, w2, b2, ln_w, ln_b, tm=256, te=256):
    t, d = x1.shape
    tm = min(tm, t)
    ne = w1g.shape[0]
    counts = counts[:, 0]
    pc = ((counts + te - 1) // te) * te
    ends = jnp.cumsum(pc)
    starts = ends - pc
    n_tiles = (t * TOP_K) // te + ne
    n_used = (ends[-1] // te).astype(I32)
    tile_ids = jnp.arange(n_tiles, dtype=I32)
    te_raw = jnp.sum((tile_ids[:, None] >= (ends // te)[None, :]).astype(I32), axis=1)
    te_raw = jnp.minimum(te_raw, ne - 1)
    tile_expert = jnp.where(tile_ids < n_used, te_raw, jnp.take(te_raw, n_used - 1))
    e_ids = jnp.arange(ne, dtype=I32)[:, None, None]
    start_of = jnp.sum(jnp.where(idx[None] == e_ids, starts.astype(I32)[:, None, None], 0), axis=0)
    pos = start_of + rank
    pos_tiles = pos.reshape(TOP_K, t // tm, tm).transpose(1, 0, 2).reshape(t // tm, 1, TOP_K * tm)
    gates = gates.T

    b1g = b1[:, None, 0::2]
    b1l = b1[:, None, 1::2]

    xs = _dispatch(x1, pos_tiles, starts.astype(I32), pc.astype(I32), n_tiles * te, tm, te)
    eo = _experts(xs, tile_expert, n_used.reshape(1), w1g, w1l, b1g, b1l,
                  w2, b2[:, None, :], te)
    return _combine(eo, pos_tiles, x1, gates, ln_w, ln_b, tm)


ATTN_GROUP = 4
ATTN_LOG_CUTOFF = -110.0


def _attn_kernel(q_ref, kt_ref, v_ref, nu_ref, o_ref, qs_ref, z0_ref, z1_ref, sp0_ref, sp1_ref,
                 w0_ref, w1_ref, acc_ref, carry_ref, *, tq):
    z_refs, sp_refs, w_refs = (z0_ref, z1_ref), (sp0_ref, sp1_ref), (w0_ref, w1_ref)
    i = pl.program_id(2)
    hd = C_HEAD_DIM
    g = ATTN_GROUP
    width = g * hd
    q = q_ref[...]
    lane = lax.broadcasted_iota(I32, (tq, width), 1)
    for h in range(g):
        in_head = (lane >= h * hd) & (lane < (h + 1) * hd)
        qs_ref[h * tq:(h + 1) * tq, :] = jnp.where(in_head, q, jnp.zeros_like(q))
    neg_u = nu_ref[...]
    row = lax.broadcasted_iota(I32, (g * tq, tq), 0) & (tq - 1)
    strict = lax.broadcasted_iota(I32, (g * tq, tq), 1) < row
    vmask = [(lane >= h * hd) & (lane < (h + 1) * hd) for h in range(g)]

    def key_rows(n):
        return pl.ds(pl.multiple_of(jnp.maximum(i - n, 0) * tq, tq), tq)

    def scores(n, slot, diag):
        z = _dot(qs_ref[...], kt_ref[jnp.maximum(i - n, 0)])
        z_refs[slot][...] = z
        zb = z.astype(BF16)
        sp = jnp.maximum(zb, 0) + jnp.log(1 + jnp.exp(-jnp.abs(zb)))
        if diag:
            sp = jnp.where(strict, sp, jnp.zeros_like(sp))
        sp_refs[slot][...] = sp

    def weights(slot, diag):
        suf = _dot(sp_refs[slot][...], neg_u)
        w = jnp.exp(z_refs[slot][...] + suf + carry_ref[...])
        if diag:
            w = jnp.where(strict, w, 0.0)
        w = w.astype(BF16)
        for h in range(g):
            w_refs[slot][:, h * tq:(h + 1) * tq] = w[h * tq:(h + 1) * tq, :]
        carry_ref[...] = carry_ref[...] + suf[:, 0:1]

    def output(n, slot, valid):
        vj = v_ref[key_rows(n), :]
        v_cat = jnp.concatenate([jnp.where(m, vj, jnp.zeros_like(vj)) for m in vmask], axis=0)
        pv = _dot(w_refs[slot][...], v_cat)
        if valid is not None:
            pv = jnp.where(valid, pv, 0.0)
        acc_ref[...] = acc_ref[...] + pv

    carry_ref[...] = jnp.zeros_like(carry_ref)
    acc_ref[...] = jnp.zeros_like(acc_ref)
    scores(0, 0, True)
    weights(0, True)
    scores(1, 1, False)
    output(0, 0, None)
    weights(1, False)

    n_groups = (i + 1) // 2
    carry_max = jnp.max(carry_ref[...])

    @pl.when((n_groups > 0) & (carry_max >= ATTN_LOG_CUTOFF))
    def _():
        scores(2, 0, False)

    def body(st):
        p, _ = st
        t = 3 + 2 * p
        output(t - 2, 1, None)
        weights(0, False)
        scores(t, 1, False)
        output(t - 1, 0, t - 1 <= i)
        weights(1, False)
        scores(t + 1, 0, False)
        return p + 1, jnp.max(carry_ref[...])

    def keep_going(st):
        p, carry_max = st
        return (p < n_groups) & (carry_max >= ATTN_LOG_CUTOFF)

    groups_done, _ = lax.while_loop(keep_going, body, (jnp.int32(0), carry_max))
    pending = 1 + 2 * groups_done
    output(pending, 1, pending <= i)
    o_ref[...] = acc_ref[...].astype(o_ref.dtype)


def _attention(qkv, bsz, seq, width, tq=256):
    tq = min(tq, seq)
    blk = ATTN_GROUP * C_HEAD_DIM
    ngroups = width // blk
    nq = seq // tq
    neg_u = -(jnp.arange(tq)[:, None] >= jnp.arange(tq)[None, :]).astype(BF16)
    qkv3 = qkv.reshape(bsz, seq, 3 * width)
    kt = qkv3[:, :, width:2 * width].reshape(bsz, nq, tq, width).transpose(0, 1, 3, 2)
    return pl.pallas_call(
        functools.partial(_attn_kernel, tq=tq),
        grid=(bsz, ngroups, nq),
        in_specs=[pl.BlockSpec((None, tq, blk), lambda b, p, i: (b, i, p)),
                  pl.BlockSpec((None, nq, blk, tq), lambda b, p, i: (b, 0, p, 0)),
                  pl.BlockSpec((None, seq, blk), lambda b, p, i: (b, 0, 2 * ngroups + p)),
                  pl.BlockSpec((tq, tq), lambda b, p, i: (0, 0))],
        out_specs=pl.BlockSpec((None, tq, blk), lambda b, p, i: (b, i, p)),
        out_shape=jax.ShapeDtypeStruct((bsz, seq, width), BF16),
        scratch_shapes=[pltpu.VMEM((ATTN_GROUP * tq, blk), BF16),
                        pltpu.VMEM((ATTN_GROUP * tq, tq), F32),
                        pltpu.VMEM((ATTN_GROUP * tq, tq), F32),
                        pltpu.VMEM((ATTN_GROUP * tq, tq), BF16),
                        pltpu.VMEM((ATTN_GROUP * tq, tq), BF16),
                        pltpu.VMEM((tq, ATTN_GROUP * tq), BF16),
                        pltpu.VMEM((tq, ATTN_GROUP * tq), BF16),
                        pltpu.VMEM((tq, blk), F32),
                        pltpu.VMEM((ATTN_GROUP * tq, 1), F32)],
        compiler_params=_params("arbitrary", "arbitrary", "arbitrary"),
        name="stickbreak_attn",
    )(qkv3, kt, qkv3, neg_u).reshape(bsz * seq, width)


def kernel(x, hgrn_lb_logits, l0_w_in, l0_a_norm_w, l0_w_pool, l0_pool_scale, l0_w_o, l0_ln1_w, l0_ln1_b, l0_w_router, l0_b_router, l0_w1, l0_b1, l0_w2, l0_b2, l0_ln2_w, l0_ln2_b, l1_w_in, l1_w_o, l1_ln1_w, l1_ln1_b, l1_w_router, l1_b_router, l1_w1, l1_b1, l1_w2, l1_b2, l1_ln2_w, l1_ln2_b):
    bsz, seq, d = x.shape
    out_dtype = x.dtype
    x2d = x.reshape(bsz * seq, d).astype(F32)
    r1 = lambda a: a.reshape(1, -1).astype(F32)

    a_width = A_HEADS * A_DK
    lb = jnp.cumsum(jax.nn.softmax(hgrn_lb_logits.astype(F32), axis=0), axis=0)[0]
    proj0 = _proj(x2d, l0_w_in.astype(BF16), F32)
    a_out = _hgrn(proj0, r1(lb), r1(l0_a_norm_w), bsz, seq)
    b_width = l0_w_pool.shape[0] * l0_w_pool.shape[1]
    b_out = _pool(proj0, l0_w_pool.astype(BF16), r1(l0_pool_scale), bsz, seq,
                  col_block=(4 * a_width) // b_width)
    x1, idx, gates, rank, counts, w1g, w1l, w2b = _mix_router(
        [a_out, b_out], l0_w_o.astype(BF16), x2d, r1(l0_ln1_w), r1(l0_ln1_b),
        l0_w_router.astype(F32), r1(l0_b_router), l0_w1, l0_w2, ROUTER_TM)
    x2d = _moe(x1, idx, gates, rank, counts, w1g, w1l, l0_b1, w2b, l0_b2,
               r1(l0_ln2_w), r1(l0_ln2_b))

    c_width = l1_w_in.shape[1] // 3
    qscale = jnp.concatenate([jnp.full((c_width,), C_HEAD_DIM ** -0.5, F32),
                              jnp.ones((2 * c_width,), F32)])
    qkv = _proj(x2d, (l1_w_in * qscale).astype(BF16), BF16)
    o = _attention(qkv, bsz, seq, c_width)
    x1, idx, gates, rank, counts, w1g, w1l, w2b = _mix_router(
        [o], l1_w_o.astype(BF16), x2d, r1(l1_ln1_w), r1(l1_ln1_b),
        l1_w_router.astype(F32), r1(l1_b_router), l1_w1, l1_w2, ROUTER_TM)
    x2d = _moe(x1, idx, gates, rank, counts, w1g, w1l, l1_b1, w2b, l1_b2,
               r1(l1_ln2_w), r1(l1_ln2_b))
    return x2d.reshape(bsz, seq, d).astype(out_dtype)
```
